```python
import math
import jax
import jax.numpy as jnp
from jax import lax
import numpy as np

D_MODEL = 2048
BATCH = 1
SEQ = 16384
DEPTH = 1
DEC_BATCH = 32
DEC_SEQ = 8
PAST_LEN = 16384
PAGE_SIZE = 128

GLA_HEADS = 4
GLA_DV = (D_MODEL // 2) // GLA_HEADS
GLA_DK = GLA_DV // 2
GATE_RANK = 16
GATE_TAU = 16.0
GLA_CHUNK = 64
DSA_HEAD_DIM = 128
DSA_HEADS = (D_MODEL // 2) // DSA_HEAD_DIM
IDX_HEADS = 16
IDX_DIM = 64
TOPK_MAX = 256
QUERY_BLOCK = 128
D_FF = 4 * D_MODEL
MIX_WIDTH = GLA_HEADS * GLA_DV + DSA_HEADS * DSA_HEAD_DIM
IN_WIDTHS = (GLA_HEADS * GLA_DK, GLA_HEADS * GLA_DK, GLA_HEADS * GLA_DV, GLA_HEADS * GLA_DV, GATE_RANK,
             DSA_HEADS * DSA_HEAD_DIM, DSA_HEADS * DSA_HEAD_DIM, DSA_HEADS * DSA_HEAD_DIM,
             IDX_HEADS * IDX_DIM, IDX_HEADS, IDX_DIM)
IN_WIDTH = sum(IN_WIDTHS)
EPS = 1e-6

kernel_name = 'hymba_gla_dsa_decoder_step'

F32 = jnp.float32


def rmsnorm(x, g):
    xf = x.astype(F32)
    y = xf * lax.rsqrt(jnp.mean(xf * xf, axis=-1, keepdims=True) + EPS)
    return (y * g.astype(F32)).astype(x.dtype)


def n_select(n_keys):
    return min(TOPK_MAX, n_keys // 4)


def take_rows(rows, idx):
    return jax.vmap(lambda r, i: r[i])(rows, idx)


def project(x, g_mix, w_in):
    B, T, _ = x.shape
    p = rmsnorm(x, g_mix) @ w_in
    offs = [int(o) for o in np.cumsum(IN_WIDTHS)[:-1]]
    gq, gk, gv, gg, ga, aq, ak, av, iq, iw, ik = jnp.split(p, offs, axis=-1)
    gla_in = (gq.reshape(B, T, GLA_HEADS, GLA_DK), gk.reshape(B, T, GLA_HEADS, GLA_DK),
              gv.reshape(B, T, GLA_HEADS, GLA_DV), gg, ga)
    dsa_in = (aq.reshape(B, T, DSA_HEADS, DSA_HEAD_DIM), ak.reshape(B, T, DSA_HEADS, DSA_HEAD_DIM),
              av.reshape(B, T, DSA_HEADS, DSA_HEAD_DIM), iq.reshape(B, T, IDX_HEADS, IDX_DIM), iw, ik)
    return gla_in, dsa_in


def gla_recurrence(q, k, v, log_a, s0):
    B, T, H, DK = q.shape
    DV = v.shape[-1]
    C = math.gcd(T, GLA_CHUNK)
    n = T // C

    def to_chunks(a):
        return a.reshape(B, n, C, H, a.shape[-1]).transpose(1, 0, 3, 2, 4)

    causal = jnp.tril(jnp.ones((C, C), dtype=bool))[:, :, None]

    def step(S, inp):
        qc, kc, vc, ac = inp
        b = jnp.cumsum(ac, axis=2)
        o_inter = jnp.einsum('bhid,bhde->bhie', qc * jnp.exp(b), S)
        diff = b[:, :, :, None, :] - b[:, :, None, :, :]
        decay = jnp.where(causal, jnp.exp(jnp.where(causal, diff, 0.0)), 0.0)
        scores = jnp.einsum('bhid,bhjd,bhijd->bhij', qc, kc, decay)
        o_intra = jnp.einsum('bhij,bhje->bhie', scores, vc)
        b_last = b[:, :, -1:, :]
        S_new = (jnp.exp(b_last[:, :, 0, :])[..., None] * S
                 + jnp.einsum('bhjd,bhje->bhde', kc * jnp.exp(b_last - b), vc))
        return S_new, o_inter + o_intra

    s_fin, o = lax.scan(step, s0, (to_chunks(q), to_chunks(k), to_chunks(v), to_chunks(log_a)))
    return o.transpose(1, 0, 3, 2, 4).reshape(B, T, H, DV), s_fin


def gla_mixer(gla_in, s0, w_gate_up, b_gate, g_gla_out):
    q, k, v, g, a_lr = gla_in
    B, T = q.shape[:2]
    log_a = jax.nn.log_sigmoid((a_lr @ w_gate_up + b_gate).astype(F32)) / GATE_TAU
    log_a = log_a.reshape(B, T, GLA_HEADS, GLA_DK)
    o, s_fin = gla_recurrence(q.astype(F32) * GLA_DK ** -0.5, k.astype(F32), v.astype(F32),
                              log_a, s0.astype(F32))
    o = rmsnorm(o, g_gla_out).reshape(B, T, GLA_HEADS * GLA_DV) * jax.nn.silu(g.astype(F32))
    return o.astype(g.dtype), s_fin


def indexer_scores(qi, wi, ki):
    dots = jnp.einsum('bthd,bsd->bths', qi.astype(F32), ki.astype(F32)) * IDX_DIM ** -0.5
    return jnp.einsum('bths,bth->bts', jax.nn.relu(dots), wi.astype(F32) * IDX_HEADS ** -0.5)


def select_keys(scores, q_pos, k_sel):
    s_pos = jnp.arange(scores.shape[-1])
    scores = jnp.where(s_pos[None, None, :] <= q_pos[None, :, None], scores, -jnp.inf)
    _, sel = lax.top_k(scores, k_sel)
    return sel, sel <= q_pos[None, :, None]


def sparse_attend(q, kg, vg, valid):
    logits = jnp.einsum('bthd,btkhd->bthk', q, kg).astype(F32) * DSA_HEAD_DIM ** -0.5
    logits = jnp.where(valid[:, :, None, :], logits, -jnp.inf)
    p = jax.nn.softmax(logits, axis=-1)
    return jnp.einsum('bthk,btkhd->bthd', p.astype(vg.dtype), vg)


def dsa_prompt(dsa_in):
    q, k, v, qi, wi, ki = dsa_in
    B, T = q.shape[:2]
    qb = math.gcd(T, QUERY_BLOCK)
    k_sel = n_select(T)

    def block(i):
        t0 = i * qb
        sl = lambda a: lax.dynamic_slice_in_dim(a, t0, qb, axis=1)
        q_pos = t0 + jnp.arange(qb)
        sel, valid = select_keys(indexer_scores(sl(qi), sl(wi), ki), q_pos, k_sel)
        return sparse_attend(sl(q), take_rows(k, sel), take_rows(v, sel), valid)

    o = lax.map(block, jnp.arange(T // qb))
    return o.transpose(1, 0, 2, 3, 4).reshape(B, T, DSA_HEADS * DSA_HEAD_DIM)


def dsa_sample(dsa_in, cache_k, cache_v, cache_idx_k, page_table, l):
    q, k, v, qi, wi, ki = dsa_in
    B, T = q.shape[:2]
    n_pages = PAST_LEN // PAGE_SIZE
    ki_past = cache_idx_k[l, page_table].reshape(B, n_pages * PAGE_SIZE, IDX_DIM)
    ki_all = jnp.concatenate([ki_past, ki], axis=1)
    q_pos = PAST_LEN + jnp.arange(T)
    sel, valid = select_keys(indexer_scores(qi, wi, ki_all), q_pos, n_select(PAST_LEN + T))
    is_past = (sel < PAST_LEN)[..., None, None]
    phys = take_rows(page_table, jnp.minimum(sel // PAGE_SIZE, n_pages - 1))
    off = sel % PAGE_SIZE
    j_new = jnp.clip(sel - PAST_LEN, 0, T - 1)
    kg = jnp.where(is_past, cache_k[l, phys, off], take_rows(k, j_new))
    vg = jnp.where(is_past, cache_v[l, phys, off], take_rows(v, j_new))
    return sparse_attend(q, kg, vg, valid).reshape(B, T, DSA_HEADS * DSA_HEAD_DIM)


def finish_layer(x, o_gla, o_dsa, w_o, g_ffn, w_up, w_down):
    x = x + jnp.concatenate([o_gla, o_dsa], axis=-1) @ w_o
    return x + jnp.square(jax.nn.relu(rmsnorm(x, g_ffn) @ w_up)) @ w_down


def setup_inputs(seed: int = 0) -> dict:
    key = jax.random.key(seed)
    ks = jax.random.split(key, 18)
    n_pages = PAST_LEN // PAGE_SIZE
    n_pool = (5 * DEC_BATCH * n_pages) // 4

    def nrm(k, shape, scale=1.0):
        return scale * jax.random.normal(k, shape, F32)

    page_table = jax.random.permutation(ks[6], n_pool)[:DEC_BATCH * n_pages]
    page_table = page_table.reshape(DEC_BATCH, n_pages).astype(jnp.int32)
    return {
        'x_prompt': nrm(ks[0], (BATCH, SEQ, D_MODEL)),
        'x_sample': nrm(ks[1], (DEC_BATCH, DEC_SEQ, D_MODEL)),
        'cache_k': nrm(ks[2], (DEPTH, n_pool, PAGE_SIZE, DSA_HEADS, DSA_HEAD_DIM)),
        'cache_v': nrm(ks[3], (DEPTH, n_pool, PAGE_SIZE, DSA_HEADS, DSA_HEAD_DIM)),
        'cache_idx_k': nrm(ks[4], (DEPTH, n_pool, PAGE_SIZE, IDX_DIM)),
        'state_gla': nrm(ks[5], (DEPTH, DEC_BATCH, GLA_HEADS, GLA_DK, GLA_DV), 0.5),
        'page_table': page_table,
        'g_mix': 1.0 + nrm(ks[7], (DEPTH, D_MODEL), 0.05),
        'w_in': nrm(ks[8], (DEPTH, D_MODEL, IN_WIDTH), D_MODEL ** -0.5),
        'w_gate_up': nrm(ks[9], (DEPTH, GATE_RANK, GLA_HEADS * GLA_DK), GATE_RANK ** -0.5),
        'b_gate': nrm(ks[10], (DEPTH, GLA_HEADS * GLA_DK), 0.1),
        'g_gla_out': 1.0 + nrm(ks[11], (DEPTH, GLA_DV), 0.05),
        'w_o': nrm(ks[12], (DEPTH, MIX_WIDTH, D_MODEL), MIX_WIDTH ** -0.5),
        'g_ffn': 1.0 + nrm(ks[13], (DEPTH, D_MODEL), 0.05),
        'w_up': nrm(ks[14], (DEPTH, D_MODEL, D_FF), D_MODEL ** -0.5),
        'w_down': nrm(ks[15], (DEPTH, D_FF, D_MODEL), D_FF ** -0.5),
        'g_final': 1.0 + nrm(ks[16], (D_MODEL,), 0.05),
    }


def reference(x_prompt, x_sample, cache_k, cache_v, cache_idx_k, state_gla, page_table,
              g_mix, w_in, w_gate_up, b_gate, g_gla_out, w_o, g_ffn, w_up, w_down, g_final):
    xp, xs = x_prompt, x_sample
    kp, vp, ip, sp, ksm, vsm, ism, ssm = [], [], [], [], [], [], [], []
    for l in range(DEPTH):
        gla_p, dsa_p = project(xp, g_mix[l], w_in[l])
        gla_s, dsa_s = project(xs, g_mix[l], w_in[l])
        s0 = jnp.zeros((xp.shape[0], GLA_HEADS, GLA_DK, GLA_DV), F32)
        o_gla_p, s_fin_p = gla_mixer(gla_p, s0, w_gate_up[l], b_gate[l], g_gla_out[l])
        o_gla_s, s_fin_s = gla_mixer(gla_s, state_gla[l], w_gate_up[l], b_gate[l], g_gla_out[l])
        o_dsa_p = dsa_prompt(dsa_p)
        o_dsa_s = dsa_sample(dsa_s, cache_k, cache_v, cache_idx_k, page_table, l)
        xp = finish_layer(xp, o_gla_p, o_dsa_p, w_o[l], g_ffn[l], w_up[l], w_down[l])
        xs = finish_layer(xs, o_gla_s, o_dsa_s, w_o[l], g_ffn[l], w_up[l], w_down[l])
        kp.append(dsa_p[1]); vp.append(dsa_p[2]); ip.append(dsa_p[5]); sp.append(s_fin_p.astype(xp.dtype))
        ksm.append(dsa_s[1]); vsm.append(dsa_s[2]); ism.append(dsa_s[5]); ssm.append(s_fin_s.astype(xs.dtype))
    y_prompt = rmsnorm(xp, g_final)
    y_sample = rmsnorm(xs, g_final)
    return (y_prompt, y_sample, jnp.stack(kp), jnp.stack(vp), jnp.stack(ip), jnp.stack(sp),
            jnp.stack(ksm), jnp.stack(vsm), jnp.stack(ism), jnp.stack(ssm))
```

```python
import functools

import jax
import jax.numpy as jnp
from jax import lax
from jax.experimental import pallas as pl
from jax.experimental.pallas import tpu as pltpu

F32 = jnp.float32
BF16 = jnp.bfloat16
I32 = jnp.int32

EPS = 1e-6
GATE_TAU = 16.0
GLA_HEADS = 4
GLA_DK = 128
GLA_DV = 256
GATE_RANK = 16
DSA_HEADS = 8
DSA_DIM = 128
IDX_HEADS = 16
IDX_DIM = 64
TOPK_MAX = 256
PAGE = 128

LANES = 128
MASKED = -1e30
M_INIT = -1e29
INT_MIN = -(2 ** 31)
VMEM_LIMIT = 52 * 1024 * 1024

C_GQ, C_GK, C_GV, C_GG = 0, 512, 1024, 2048
C_AQ, C_AK, C_AV, C_IQ = 3072, 4096, 5120, 6144
C_SMALL = 7168
SM_IK, SM_GA, SM_IW = 0, 64, 80
P_WIDTH = 7680


def _cparams(sem):
    return pltpu.CompilerParams(dimension_semantics=sem, vmem_limit_bytes=VMEM_LIMIT)


def _norm_mm_kernel(x_ref, g_ref, w_ref, o_ref, h_ref):
    @pl.when(pl.program_id(1) == 0)
    def _():
        x = x_ref[...]
        ms = jnp.mean(x * x, axis=-1, keepdims=True)
        h_ref[...] = (x * lax.rsqrt(ms + EPS) * g_ref[...]).astype(BF16)

    o_ref[...] = jnp.dot(h_ref[...], w_ref[...], preferred_element_type=F32)


def norm_matmul(x, g, w, tm, tn):
    M, D = x.shape
    N = w.shape[1]
    return pl.pallas_call(
        _norm_mm_kernel,
        grid=(M // tm, N // tn),
        in_specs=[pl.BlockSpec((tm, D), lambda i, j: (i, 0)),
                  pl.BlockSpec((1, D), lambda i, j: (0, 0)),
                  pl.BlockSpec((D, tn), lambda i, j: (0, j))],
        out_specs=pl.BlockSpec((tm, tn), lambda i, j: (i, j)),
        out_shape=jax.ShapeDtypeStruct((M, N), F32),
        scratch_shapes=[pltpu.VMEM((tm, D), BF16)],
        compiler_params=_cparams(("parallel", "arbitrary")),
        name="norm_matmul",
    )(x, g.reshape(1, D), w)


def _gla_kernel(q_ref, k_ref, v_ref, g_ref, sm_ref, wg_ref, bg_ref, go_ref, s0_ref,
                o_ref, sfin_ref, st_ref, b_ref, oacc_ref, *, tb, C):
    H, DK, DV = GLA_HEADS, GLA_DK, GLA_DV
    t = pl.program_id(1)

    @pl.when(t == 0)
    def _():
        for h in range(H):
            st_ref[h] = s0_ref[0, h].T

    a_lr = sm_ref[:, SM_GA:SM_GA + GATE_RANK]
    z = jnp.dot(a_lr.astype(BF16), wg_ref[...], preferred_element_type=F32) + bg_ref[...]
    log_a = (jnp.minimum(z, 0.0) - jnp.log1p(jnp.exp(-jnp.abs(z)))) / GATE_TAU
    row = lax.broadcasted_iota(I32, (tb, H * DK), 0) & (C - 1)
    b = log_a
    sh = 1
    while sh < C:
        b = b + jnp.where(row >= sh, pltpu.roll(b, sh, axis=0), 0.0)
        sh *= 2
    b_ref[...] = b

    jj = lax.broadcasted_iota(I32, (C, 1), 0)

    def chunk(c, carry):
        r0 = pl.multiple_of(c * C, C)
        for h in range(H):
            bq = b_ref[pl.ds(r0, C), h * DK:(h + 1) * DK]
            q = q_ref[pl.ds(r0, C), h * DK:(h + 1) * DK] * (DK ** -0.5)
            k = k_ref[pl.ds(r0, C), h * DK:(h + 1) * DK]
            v = v_ref[pl.ds(r0, C), h * DV:(h + 1) * DV]
            st = st_ref[h]
            qe = q * jnp.exp(bq)
            o = lax.dot_general(qe.astype(BF16), st.astype(BF16), (((1,), (1,)), ((), ())),
                                preferred_element_type=F32)
            rows = []
            for i in range(C):
                causal = jj <= i
                d = jnp.where(causal, bq[i:i + 1, :] - bq, 0.0)
                a = jnp.sum(q[i:i + 1, :] * k * jnp.exp(d), axis=-1, keepdims=True)
                a = jnp.where(causal, a, 0.0)
                rows.append(jnp.sum(a * v, axis=0, keepdims=True))
            o = o + jnp.concatenate(rows, axis=0)
            oacc_ref[pl.ds(r0, C), h * DV:(h + 1) * DV] = o
            b_last = bq[C - 1:C, :]
            ke = k * jnp.exp(b_last - bq)
            kv_t = lax.dot_general(v.astype(BF16), ke.astype(BF16), (((0,), (0,)), ((), ())),
                                   preferred_element_type=F32)
            st_ref[h] = st * jnp.exp(b_last) + kv_t
        return carry

    lax.fori_loop(0, tb // C, chunk, 0)

    gate = g_ref[...]
    gate = gate * (1.0 / (1.0 + jnp.exp(-gate)))
    for h in range(H):
        o = oacc_ref[:, h * DV:(h + 1) * DV]
        ms = jnp.mean(o * o, axis=-1, keepdims=True)
        y = o * lax.rsqrt(ms + EPS) * go_ref[...]
        o_ref[:, h * DV:(h + 1) * DV] = (y * gate[:, h * DV:(h + 1) * DV]).astype(o_ref.dtype)

    @pl.when(t == pl.num_programs(1) - 1)
    def _():
        for h in range(H):
            sfin_ref[0, h] = st_ref[h].T


def gla(p, w_gate_up, b_gate, g_gla_out, s0, B, T, tb, C):
    H, DK, DV = GLA_HEADS, GLA_DK, GLA_DV
    nt = T // tb
    row = lambda b, t: b * nt + t
    return pl.pallas_call(
        functools.partial(_gla_kernel, tb=tb, C=C),
        grid=(B, nt),
        in_specs=[pl.BlockSpec((tb, H * DK), lambda b, t: (row(b, t), C_GQ // (H * DK))),
                  pl.BlockSpec((tb, H * DK), lambda b, t: (row(b, t), C_GK // (H * DK))),
                  pl.BlockSpec((tb, H * DV), lambda b, t: (row(b, t), C_GV // (H * DV))),
                  pl.BlockSpec((tb, H * DV), lambda b, t: (row(b, t), C_GG // (H * DV))),
                  pl.BlockSpec((tb, LANES), lambda b, t: (row(b, t), C_SMALL // LANES)),
                  pl.BlockSpec((GATE_RANK, H * DK), lambda b, t: (0, 0)),
                  pl.BlockSpec((1, H * DK), lambda b, t: (0, 0)),
                  pl.BlockSpec((1, DV), lambda b, t: (0, 0)),
                  pl.BlockSpec((1, H, DK, DV), lambda b, t: (b, 0, 0, 0))],
        out_specs=[pl.BlockSpec((tb, H * DV), lambda b, t: (row(b, t), 0)),
                   pl.BlockSpec((1, H, DK, DV), lambda b, t: (b, 0, 0, 0))],
        out_shape=[jax.ShapeDtypeStruct((B * T, H * DV), BF16),
                   jax.ShapeDtypeStruct((B, H, DK, DV), F32)],
        scratch_shapes=[pltpu.VMEM((H, DV, DK), F32),
                        pltpu.VMEM((tb, H * DK), F32),
                        pltpu.VMEM((tb, H * DV), F32)],
        compiler_params=_cparams(("arbitrary", "arbitrary")),
        name="gla",
    )(p, p, p, p, p, w_gate_up.astype(BF16), b_gate.reshape(1, -1), g_gla_out.reshape(1, -1), s0)


def _sortable_key(score):
    bits = lax.bitcast_convert_type(score + 0.0, I32)
    return bits ^ ((bits >> 31) & 0x7FFFFFFF)


def _select_threshold(key_ref, n_chunks, ktarget, pos0, rows):
    ck = key_ref.shape[-1]

    def count(pred):
        def body(c, cnt):
            blk = key_ref[c]
            pos = pos0(c) + lax.broadcasted_iota(I32, (rows, ck), 1)
            m = jnp.where(pred(blk, pos), 1, 0)
            for n in range(ck // LANES):
                cnt = cnt + m[:, n * LANES:(n + 1) * LANES]
            return cnt
        cnt = lax.fori_loop(0, n_chunks, body, jnp.zeros((rows, LANES), I32))
        return jnp.sum(cnt, axis=-1, keepdims=True)

    def count_ge(thr):
        return count(lambda blk, pos: blk >= thr)

    thr = jnp.where(count_ge(jnp.zeros((rows, 1), I32)) >= ktarget, 0, INT_MIN).astype(I32)

    def bit_step(it, thr):
        cand = thr | jnp.left_shift(jnp.int32(1), 30 - it)
        return jnp.where(count_ge(cand) >= ktarget, cand, thr)

    thr = lax.fori_loop(0, 31, bit_step, thr)
    n_gt = count(lambda blk, pos: blk > thr)
    n_ge = count_ge(thr)
    need = ktarget - n_gt
    extra = jnp.max(n_ge - n_gt - need)

    def tie_cut():
        def step(_, lohi):
            lo, hi = lohi
            mid = lo + ((hi - lo) >> 1)
            c = count(lambda blk, pos: (blk == thr) & (pos <= mid))
            ok = c >= need
            return jnp.where(ok, lo, mid), jnp.where(ok, mid, hi)
        lo = jnp.full((rows, 1), -1, I32)
        hi = jnp.full((rows, 1), 2 ** 30, I32)
        _, hi = lax.fori_loop(0, 31, step, (lo, hi))
        return hi

    cut = lax.cond(extra > 0, tie_cut, lambda: jnp.full((rows, 1), 2 ** 30, I32))
    return thr, cut


def _idx_kernel(iq_ref, sm_ref, kt2_ref, bias_ref, key_ref, wb_ref, *, tq, ck, sub, k_sel):
    i = pl.program_id(0)
    n_all = bias_ref.shape[0]
    n_chunks = ((i + 1) * tq + ck - 1) // ck
    nsub = ck // sub
    t_pos = i * tq + lax.broadcasted_iota(I32, (tq, 1), 0)

    w = sm_ref[:, SM_IW:SM_IW + IDX_HEADS] * (IDX_HEADS ** -0.5 * IDX_DIM ** -0.5)
    for h in range(IDX_HEADS):
        wb_ref[h] = jnp.broadcast_to(w[:, h:h + 1], (tq, LANES))
    q = iq_ref[...].astype(BF16)

    def score_chunk(c, carry):
        for u in range(nsub):
            rhs = kt2_ref[c * nsub + u]
            acc = [jnp.zeros((tq, LANES), F32) for _ in range(sub // LANES)]
            for hp in range(IDX_HEADS // 2):
                d = jnp.dot(q[:, hp * 128:(hp + 1) * 128], rhs, preferred_element_type=F32)
                d = jnp.maximum(d, 0.0)
                for n in range(sub // LANES):
                    acc[n] = acc[n] + d[:, n * LANES:(n + 1) * LANES] * wb_ref[2 * hp]
                    acc[n] = acc[n] + d[:, sub + n * LANES:sub + (n + 1) * LANES] * wb_ref[2 * hp + 1]
            for n in range(sub // LANES):
                s_pos = c * ck + u * sub + n * LANES + lax.broadcasted_iota(I32, (tq, LANES), 1)
                sc = jnp.where(s_pos <= t_pos, acc[n], -jnp.inf)
                key_ref[c, :, u * sub + n * LANES:u * sub + (n + 1) * LANES] = _sortable_key(sc)
        return carry

    lax.fori_loop(0, n_chunks, score_chunk, 0)

    ktarget = jnp.minimum(t_pos + 1, k_sel)
    thr, cut = _select_threshold(key_ref, n_chunks, ktarget, lambda c: c * ck, tq)

    def write_sel(c, carry):
        blk = key_ref[c]
        pos = c * ck + lax.broadcasted_iota(I32, (tq, ck), 1)
        sel = (blk > thr) | ((blk == thr) & (pos <= cut))
        bias_ref[c] = jnp.where(sel, 0.0, MASKED).astype(bias_ref.dtype)
        return carry

    lax.fori_loop(0, n_chunks, write_sel, 0)

    def write_rest(c, carry):
        bias_ref[c] = jnp.full((tq, ck), MASKED, bias_ref.dtype)
        return carry

    lax.fori_loop(n_chunks, n_all, write_rest, 0)


def _blockdiag_keys(ki, sub):
    S = ki.shape[0]
    kt = ki.astype(BF16).reshape(S // sub, sub, IDX_DIM).transpose(0, 2, 1)
    z = jnp.zeros_like(kt)
    return jnp.concatenate([jnp.concatenate([kt, z], axis=2), jnp.concatenate([z, kt], axis=2)], axis=1)


def idx_select(p, T, tq, ck, sub):
    k_sel = min(TOPK_MAX, T // 4)
    kt2 = _blockdiag_keys(p[:, C_SMALL + SM_IK:C_SMALL + SM_IK + IDX_DIM], sub)
    n_all = T // ck
    return pl.pallas_call(
        functools.partial(_idx_kernel, tq=tq, ck=ck, sub=sub, k_sel=k_sel),
        grid=(T // tq,),
        in_specs=[pl.BlockSpec((tq, IDX_HEADS * IDX_DIM), lambda i: (i, C_IQ // (IDX_HEADS * IDX_DIM))),
                  pl.BlockSpec((tq, LANES), lambda i: (i, C_SMALL // LANES)),
                  pl.BlockSpec((T // sub, 128, 2 * sub), lambda i: (0, 0, 0))],
        out_specs=pl.BlockSpec((n_all, tq, ck), lambda i: (0, i, 0)),
        out_shape=jax.ShapeDtypeStruct((n_all, T, ck), BF16),
        scratch_shapes=[pltpu.VMEM((n_all, tq, ck), I32),
                        pltpu.VMEM((IDX_HEADS, tq, LANES), F32)],
        compiler_params=_cparams(("arbitrary",)),
        name="idx_select",
    )(p, p, kt2)


def _attn_kernel(q_ref, k_ref, v_ref, b_ref, o_ref, m_ref, l_ref, acc_ref, *, tq, tk):
    H, D = DSA_HEADS, DSA_DIM
    i, j = pl.program_id(0), pl.program_id(1)

    @pl.when(j == 0)
    def _():
        m_ref[...] = jnp.full(m_ref.shape, M_INIT, F32)
        l_ref[...] = jnp.zeros(l_ref.shape, F32)
        acc_ref[...] = jnp.zeros(acc_ref.shape, F32)

    @pl.when(j * tk <= i * tq + tq - 1)
    def _():
        bias = b_ref[0].astype(F32)
        for h in range(H):
            q = q_ref[:, h * D:(h + 1) * D]
            k = k_ref[:, h * D:(h + 1) * D]
            v = v_ref[:, h * D:(h + 1) * D]
            s = lax.dot_general(q, k, (((1,), (1,)), ((), ())), preferred_element_type=F32)
            s = s * (D ** -0.5) + bias
            m_prev = m_ref[h]
            m_new = jnp.maximum(m_prev, jnp.max(s, axis=-1, keepdims=True))
            alpha = jnp.exp(m_prev - m_new)
            pr = jnp.exp(s - m_new)
            l_ref[h] = alpha * l_ref[h] + jnp.sum(pr, axis=-1, keepdims=True)
            acc_ref[:, h * D:(h + 1) * D] = alpha * acc_ref[:, h * D:(h + 1) * D] + jnp.dot(
                pr.astype(BF16), v, preferred_element_type=F32)
            m_ref[h] = m_new

    @pl.when(j == pl.num_programs(1) - 1)
    def _():
        for h in range(H):
            o_ref[:, h * D:(h + 1) * D] = (acc_ref[:, h * D:(h + 1) * D] / l_ref[h]).astype(o_ref.dtype)


def attention(q, k, v, bias3, T, tq, tk):
    H, D = DSA_HEADS, DSA_DIM
    last = lambda i, j: jnp.minimum(j, (i * tq + tq - 1) // tk)
    return pl.pallas_call(
        functools.partial(_attn_kernel, tq=tq, tk=tk),
        grid=(T // tq, T // tk),
        in_specs=[pl.BlockSpec((tq, H * D), lambda i, j: (i, 0)),
                  pl.BlockSpec((tk, H * D), lambda i, j: (last(i, j), 0)),
                  pl.BlockSpec((tk, H * D), lambda i, j: (last(i, j), 0)),
                  pl.BlockSpec((1, tq, tk), lambda i, j: (last(i, j), i, 0))],
        out_specs=pl.BlockSpec((tq, H * D), lambda i, j: (i, 0)),
        out_shape=jax.ShapeDtypeStruct((T, H * D), BF16),
        scratch_shapes=[pltpu.VMEM((H, tq, 1), F32), pltpu.VMEM((H, tq, 1), F32),
                        pltpu.VMEM((tq, H * D), F32)],
        compiler_params=_cparams(("parallel", "arbitrary")),
        name="attention",
    )(q, k, v, bias3)


def _idx_sample_kernel(pt_ref, q_ref, w_ref, kn_ref, cache_ref, bias_ref, kbuf_ref, key_ref, sem, *,
                       n_pages, ck, k_sel, t_dec):
    b = pl.program_id(0)
    n_chunks = key_ref.shape[0]
    ppc = ck // PAGE

    def page_copy(pg):
        return pltpu.make_async_copy(cache_ref.at[pt_ref[b, pg]], kbuf_ref.at[pl.ds(pg * PAGE, PAGE)], sem)

    def start(pg, c):
        page_copy(pg).start()
        return c

    lax.fori_loop(0, n_pages, start, 0)
    kbuf_ref[pl.ds(n_pages * PAGE, ck)] = jnp.concatenate(
        [kn_ref[0], jnp.zeros((ck - t_dec, IDX_DIM), F32)], axis=0)

    def wait(pg, c):
        page_copy(pg).wait()
        return c

    lax.fori_loop(0, n_pages, wait, 0)

    q = q_ref[0]
    wcol = w_ref[0]
    t_pos = n_pages * PAGE + lax.broadcasted_iota(I32, (t_dec, 1), 0)

    def score_chunk(c, carry):
        kc = kbuf_ref[pl.ds(pl.multiple_of(c * ck, ck), ck)].astype(BF16)
        d = lax.dot_general(q, kc, (((1,), (1,)), ((), ())), preferred_element_type=F32)
        d = jnp.maximum(d, 0.0) * wcol
        sc = d[0:t_dec]
        for h in range(1, IDX_HEADS):
            sc = sc + d[h * t_dec:(h + 1) * t_dec]
        s_pos = c * ck + lax.broadcasted_iota(I32, (t_dec, ck), 1)
        key_ref[c] = _sortable_key(jnp.where(s_pos <= t_pos, sc, -jnp.inf))
        return carry

    lax.fori_loop(0, n_chunks, score_chunk, 0)

    ktarget = jnp.full((t_dec, 1), k_sel, I32)
    thr, cut = _select_threshold(key_ref, n_chunks, ktarget, lambda c: c * ck, t_dec)

    def write_sel(c, carry):
        blk = key_ref[c]
        pos = c * ck + lax.broadcasted_iota(I32, (t_dec, ck), 1)
        sel = (blk > thr) | ((blk == thr) & (pos <= cut))
        bias_ref[0, c] = jnp.where(sel, 0.0, MASKED)
        return carry

    lax.fori_loop(0, n_chunks, write_sel, 0)


def idx_select_sample(ps, cache_idx_k, page_table, B, t_dec, ck):
    n_pages = page_table.shape[1]
    past = n_pages * PAGE
    k_sel = min(TOPK_MAX, (past + t_dec) // 4)
    n_chunks = past // ck + 1
    iq = ps[:, C_IQ:C_IQ + IDX_HEADS * IDX_DIM].reshape(B, t_dec, IDX_HEADS, IDX_DIM)
    iq = iq.transpose(0, 2, 1, 3).reshape(B, IDX_HEADS * t_dec, IDX_DIM).astype(BF16)
    iw = ps[:, C_SMALL + SM_IW:C_SMALL + SM_IW + IDX_HEADS].reshape(B, t_dec, IDX_HEADS)
    iw = (iw * (IDX_HEADS ** -0.5 * IDX_DIM ** -0.5)).transpose(0, 2, 1).reshape(B, IDX_HEADS * t_dec, 1)
    kn = ps[:, C_SMALL + SM_IK:C_SMALL + SM_IK + IDX_DIM].reshape(B, t_dec, IDX_DIM)
    grid_spec = pltpu.PrefetchScalarGridSpec(
        num_scalar_prefetch=1,
        grid=(B,),
        in_specs=[pl.BlockSpec((1, IDX_HEADS * t_dec, IDX_DIM), lambda b, pt: (b, 0, 0)),
                  pl.BlockSpec((1, IDX_HEADS * t_dec, 1), lambda b, pt: (b, 0, 0)),
                  pl.BlockSpec((1, t_dec, IDX_DIM), lambda b, pt: (b, 0, 0)),
                  pl.BlockSpec(memory_space=pl.ANY)],
        out_specs=pl.BlockSpec((1, n_chunks, t_dec, ck), lambda b, pt: (b, 0, 0, 0)),
        scratch_shapes=[pltpu.VMEM((past + ck, IDX_DIM), F32),
                        pltpu.VMEM((n_chunks, t_dec, ck), I32),
                        pltpu.SemaphoreType.DMA(())],
    )
    return pl.pallas_call(
        functools.partial(_idx_sample_kernel, n_pages=n_pages, ck=ck, k_sel=k_sel, t_dec=t_dec),
        grid_spec=grid_spec,
        out_shape=jax.ShapeDtypeStruct((B, n_chunks, t_dec, ck), F32),
        compiler_params=_cparams(("arbitrary",)),
        name="idx_select_sample",
    )(page_table, iq, iw, kn, cache_idx_k)


def _attn_sample_kernel(pt_ref, q_ref, kn_ref, vn_ref, b_ref, hm_ref, *refs, pg, t_dec):
    H, D = DSA_HEADS, DSA_DIM
    k_refs, v_refs = refs[:pg], refs[pg:2 * pg]
    o_ref, m_ref, l_ref, acc_ref = refs[2 * pg:]
    j = pl.program_id(1)
    nj = pl.num_programs(1)

    @pl.when(j == 0)
    def _():
        m_ref[...] = jnp.full(m_ref.shape, M_INIT, F32)
        l_ref[...] = jnp.zeros(l_ref.shape, F32)
        acc_ref[...] = jnp.zeros(acc_ref.shape, F32)

    def update(kc, vc):
        bias = b_ref[0, 0].astype(F32)
        s = lax.dot_general(q_ref[0], kc, (((1,), (1,)), ((), ())), preferred_element_type=F32)
        s = s * (D ** -0.5) + jnp.concatenate([bias] * H, axis=0) + hm_ref[...]
        m_prev = m_ref[...]
        m_new = jnp.maximum(m_prev, jnp.max(s, axis=-1, keepdims=True))
        alpha = jnp.exp(m_prev - m_new)
        pr = jnp.exp(s - m_new)
        l_ref[...] = alpha * l_ref[...] + jnp.sum(pr, axis=-1, keepdims=True)
        acc_ref[...] = alpha * acc_ref[...] + jnp.dot(pr.astype(BF16), vc, preferred_element_type=F32)
        m_ref[...] = m_new

    @pl.when(j < nj - 1)
    def _():
        kc = jnp.concatenate([r[0] for r in k_refs], axis=0).astype(BF16)
        vc = jnp.concatenate([r[0] for r in v_refs], axis=0).astype(BF16)
        update(kc, vc)

    @pl.when(j == nj - 1)
    def _():
        pad = jnp.zeros(((pg * PAGE - t_dec) * H, D), F32)
        kc = jnp.concatenate([kn_ref[0], pad], axis=0).astype(BF16)
        vc = jnp.concatenate([vn_ref[0], pad], axis=0).astype(BF16)
        update(kc, vc)
        out = acc_ref[...] / l_ref[...]
        for h in range(H):
            o_ref[0, :, h * D:(h + 1) * D] = out[h * t_dec:(h + 1) * t_dec].astype(o_ref.dtype)


def attention_sample(ps, cache_k, cache_v, page_table, bias, B, t_dec, pg):
    H, D = DSA_HEADS, DSA_DIM
    n_pages = page_table.shape[1]
    nj = n_pages // pg + 1
    q = ps[:, C_AQ:C_AQ + H * D].reshape(B, t_dec, H, D).transpose(0, 2, 1, 3)
    q = q.reshape(B, H * t_dec, D).astype(BF16)
    kn = ps[:, C_AK:C_AK + H * D].reshape(B, t_dec * H, D)
    vn = ps[:, C_AV:C_AV + H * D].reshape(B, t_dec * H, D)
    nk = pg * PAGE * H
    q_head = jnp.arange(H * t_dec, dtype=I32)[:, None] // t_dec
    k_head = jnp.arange(nk, dtype=I32)[None, :] % H
    head_mask = jnp.where(q_head == k_head, 0.0, MASKED).astype(F32)

    def page_map(u):
        return lambda b, j, pt: (pt[b, jnp.minimum(j, n_pages // pg - 1) * pg + u], 0, 0)

    page_specs = [pl.BlockSpec((1, PAGE * H, D), page_map(u)) for u in range(pg)]
    grid_spec = pltpu.PrefetchScalarGridSpec(
        num_scalar_prefetch=1,
        grid=(B, nj),
        in_specs=[pl.BlockSpec((1, H * t_dec, D), lambda b, j, pt: (b, 0, 0)),
                  pl.BlockSpec((1, t_dec * H, D), lambda b, j, pt: (b, 0, 0)),
                  pl.BlockSpec((1, t_dec * H, D), lambda b, j, pt: (b, 0, 0)),
                  pl.BlockSpec((1, 1, t_dec, nk), lambda b, j, pt: (b, j, 0, 0)),
                  pl.BlockSpec((H * t_dec, nk), lambda b, j, pt: (0, 0))]
        + page_specs + page_specs,
        out_specs=pl.BlockSpec((1, t_dec, H * D), lambda b, j, pt: (b, 0, 0)),
        scratch_shapes=[pltpu.VMEM((H * t_dec, 1), F32), pltpu.VMEM((H * t_dec, 1), F32),
                        pltpu.VMEM((H * t_dec, D), F32)],
    )
    return pl.pallas_call(
        functools.partial(_attn_sample_kernel, pg=pg, t_dec=t_dec),
        grid_spec=grid_spec,
        out_shape=jax.ShapeDtypeStruct((B, t_dec, H * D), F32),
        compiler_params=_cparams(("parallel", "arbitrary")),
        name="attention_sample",
    )(page_table, q, kn, vn, bias, head_mask, *([cache_k] * pg), *([cache_v] * pg))


def _out_proj_kernel(x_ref, a_ref, b_ref, wa_ref, wb_ref, o_ref):
    o_ref[...] = (x_ref[...] + jnp.dot(a_ref[...], wa_ref[...], preferred_element_type=F32)
                  + jnp.dot(b_ref[...], wb_ref[...], preferred_element_type=F32))


def out_proj(x, a, b, wa, wb, tm, tn):
    M, D = x.shape
    Ka, Kb = a.shape[1], b.shape[1]
    return pl.pallas_call(
        _out_proj_kernel,
        grid=(M // tm, D // tn),
        in_specs=[pl.BlockSpec((tm, tn), lambda i, j: (i, j)),
                  pl.BlockSpec((tm, Ka), lambda i, j: (i, 0)),
                  pl.BlockSpec((tm, Kb), lambda i, j: (i, 0)),
                  pl.BlockSpec((Ka, tn), lambda i, j: (0, j)),
                  pl.BlockSpec((Kb, tn), lambda i, j: (0, j))],
        out_specs=pl.BlockSpec((tm, tn), lambda i, j: (i, j)),
        out_shape=jax.ShapeDtypeStruct((M, D), F32),
        compiler_params=_cparams(("parallel", "arbitrary")),
        name="out_proj",
    )(x, a, b, wa, wb)


def _ffn_kernel(x_ref, g_ref, wu_ref, wd_ref, gf_ref, o_ref, h_ref, acc_ref):
    f = pl.program_id(1)

    @pl.when(f == 0)
    def _():
        x = x_ref[...]
        ms = jnp.mean(x * x, axis=-1, keepdims=True)
        h_ref[...] = (x * lax.rsqrt(ms + EPS) * g_ref[...]).astype(BF16)
        acc_ref[...] = jnp.zeros(acc_ref.shape, F32)

    u = jnp.dot(h_ref[...], wu_ref[...], preferred_element_type=F32)
    u = jnp.square(jnp.maximum(u, 0.0)).astype(BF16)
    acc_ref[...] += jnp.dot(u, wd_ref[...], preferred_element_type=F32)

    @pl.when(f == pl.num_programs(1) - 1)
    def _():
        x2 = x_ref[...] + acc_ref[...]
        ms = jnp.mean(x2 * x2, axis=-1, keepdims=True)
        o_ref[...] = x2 * lax.rsqrt(ms + EPS) * gf_ref[...]


def ffn(x, g_ffn, w_up, w_down, g_final, tm, tf):
    M, D = x.shape
    Fd = w_up.shape[1]
    return pl.pallas_call(
        _ffn_kernel,
        grid=(M // tm, Fd // tf),
        in_specs=[pl.BlockSpec((tm, D), lambda i, f: (i, 0)),
                  pl.BlockSpec((1, D), lambda i, f: (0, 0)),
                  pl.BlockSpec((D, tf), lambda i, f: (0, f)),
                  pl.BlockSpec((tf, D), lambda i, f: (f, 0)),
                  pl.BlockSpec((1, D), lambda i, f: (0, 0))],
        out_specs=pl.BlockSpec((tm, D), lambda i, f: (i, 0)),
        out_shape=jax.ShapeDtypeStruct((M, D), F32),
        scratch_shapes=[pltpu.VMEM((tm, D), BF16), pltpu.VMEM((tm, D), F32)],
        compiler_params=_cparams(("parallel", "arbitrary")),
        name="ffn",
    )(x, g_ffn.reshape(1, D), w_up, w_down, g_final.reshape(1, D))


def _pack_w_in(w_in):
    D = w_in.shape[0]
    gq, gk, gv, gg, ga, aq, ak, av, iq, iw, ik = jnp.split(
        w_in, [512, 1024, 2048, 3072, 3088, 4112, 5136, 6160, 7184, 7200], axis=1)
    pad = jnp.zeros((D, P_WIDTH - C_SMALL - 96), w_in.dtype)
    return jnp.concatenate([gq, gk, gv, gg, aq, ak, av, iq, ik, ga, iw, pad], axis=1).astype(BF16)


def _tile(n, pref):
    return pref if n % pref == 0 else n


def kernel(x_prompt, x_sample, cache_k, cache_v, cache_idx_k, state_gla, page_table,
           g_mix, w_in, w_gate_up, b_gate, g_gla_out, w_o, g_ffn, w_up, w_down, g_final):
    depth = w_in.shape[0]
    Bp, Tp, D = x_prompt.shape
    Bs, Ts, _ = x_sample.shape
    assert Bp == 1 and depth == 1
    H, Dh = DSA_HEADS, DSA_DIM
    n_pool = cache_k.shape[1]

    xp = x_prompt.reshape(Bp * Tp, D)
    xs = x_sample.reshape(Bs * Ts, D)
    l = 0
    w_pack = _pack_w_in(w_in[l])
    wo_a = w_o[l, :GLA_HEADS * GLA_DV].astype(BF16)
    wo_b = w_o[l, GLA_HEADS * GLA_DV:].astype(BF16)
    wu = w_up[l].astype(BF16)
    wd = w_down[l].astype(BF16)

    pp = norm_matmul(xp, g_mix[l], w_pack, _tile(Tp, 512), 1536)
    s0 = jnp.zeros((Bp, GLA_HEADS, GLA_DK, GLA_DV), F32)
    o_gla_p, s_fin_p = gla(pp, w_gate_up[l], b_gate[l], g_gla_out[l], s0, Bp, Tp, _tile(Tp, 128), 16)
    ck = 512
    bias3 = idx_select(pp, Tp, 128, ck, 256)
    k_p = pp[:, C_AK:C_AK + H * Dh]
    v_p = pp[:, C_AV:C_AV + H * Dh]
    o_dsa_p = attention(pp[:, C_AQ:C_AQ + H * Dh].astype(BF16), k_p.astype(BF16), v_p.astype(BF16),
                        bias3, Tp, 256, ck)
    x1p = out_proj(xp, o_gla_p, o_dsa_p, wo_a, wo_b, _tile(Tp, 512), 1024)
    y_p = ffn(x1p, g_ffn[l], wu, wd, g_final, _tile(Tp, 512), 512)

    ps = norm_matmul(xs, g_mix[l], w_pack, Bs * Ts, 1536)
    o_gla_s, s_fin_s = gla(ps, w_gate_up[l], b_gate[l], g_gla_out[l], state_gla[l], Bs, Ts, Ts, Ts)
    pg = 4
    bias_s = idx_select_sample(ps, cache_idx_k[l], page_table, Bs, Ts, pg * PAGE)
    bias_s = jnp.repeat(bias_s.astype(BF16), H, axis=-1)
    o_dsa_s = attention_sample(ps, cache_k[l].reshape(n_pool, PAGE * H, Dh), cache_v[l].reshape(n_pool, PAGE * H, Dh),
                               page_table, bias_s, Bs, Ts, pg)
    x1s = out_proj(xs, o_gla_s, o_dsa_s.reshape(Bs * Ts, H * Dh).astype(BF16), wo_a, wo_b, Bs * Ts, 1024)
    y_s = ffn(x1s, g_ffn[l], wu, wd, g_final, Bs * Ts, 512)

    return (y_p.reshape(Bp, Tp, D), y_s.reshape(Bs, Ts, D),
            k_p.reshape(1, Bp, Tp, H, Dh), v_p.reshape(1, Bp, Tp, H, Dh),
            pp[:, C_SMALL + SM_IK:C_SMALL + SM_IK + IDX_DIM].reshape(1, Bp, Tp, IDX_DIM),
            s_fin_p[None],
            ps[:, C_AK:C_AK + H * Dh].reshape(1, Bs, Ts, H, Dh),
            ps[:, C_AV:C_AV + H * Dh].reshape(1, Bs, Ts, H, Dh),
            ps[:, C_SMALL + SM_IK:C_SMALL + SM_IK + IDX_DIM].reshape(1, Bs, Ts, IDX_DIM),
            s_fin_s[None])
```

```python
import functools

import jax
import jax.numpy as jnp
from jax import lax
from jax.experimental import pallas as pl
from jax.experimental.pallas import tpu as pltpu

F32 = jnp.float32
BF16 = jnp.bfloat16
I32 = jnp.int32
I16 = jnp.int16

EPS = 1e-6
GATE_TAU = 16.0
GLA_HEADS = 4
GLA_DK = 128
GLA_DV = 256
GATE_RANK = 16
DSA_HEADS = 8
DSA_DIM = 128
IDX_HEADS = 16
IDX_DIM = 64
TOPK_MAX = 256
PAGE = 128

LANES = 128
MASKED = -1e30
M_INIT = -1e29
INT_MIN = -(2 ** 31)
VMEM_LIMIT = 52 * 1024 * 1024
IDX_VMEM_LIMIT = 56 * 1024 * 1024

C_GQ, C_GK, C_GV, C_GG = 0, 512, 1024, 2048
C_SMALL = 3072
SM_IK, SM_GA, SM_IW = 0, 64, 80
PA_WIDTH = 3328
CT_AQ, CT_AV, CT_IQ = 0, 1024, 2048
CT_ROWS = 3072


def _cparams(sem):
    return pltpu.CompilerParams(dimension_semantics=sem, vmem_limit_bytes=VMEM_LIMIT)


def _norm_mm_kernel(x_ref, g_ref, w_ref, o_ref, h_ref):
    @pl.when(pl.program_id(1) == 0)
    def _():
        x = x_ref[...]
        ms = jnp.mean(x * x, axis=-1, keepdims=True)
        h_ref[...] = (x * lax.rsqrt(ms + EPS) * g_ref[...]).astype(BF16)

    o_ref[...] = jnp.dot(h_ref[...], w_ref[...], preferred_element_type=F32)


def norm_matmul(x, g, w, tm, tn):
    M, D = x.shape
    N = w.shape[1]
    return pl.pallas_call(
        _norm_mm_kernel,
        grid=(M // tm, N // tn),
        in_specs=[pl.BlockSpec((tm, D), lambda i, j: (i, 0)),
                  pl.BlockSpec((1, D), lambda i, j: (0, 0)),
                  pl.BlockSpec((D, tn), lambda i, j: (0, j))],
        out_specs=pl.BlockSpec((tm, tn), lambda i, j: (i, j)),
        out_shape=jax.ShapeDtypeStruct((M, N), F32),
        scratch_shapes=[pltpu.VMEM((tm, D), BF16)],
        compiler_params=_cparams(("parallel", "arbitrary")),
        name="norm_matmul",
    )(x, g.reshape(1, D), w)


def _norm_mm_heads_kernel(x_ref, g_ref, w_ref, k3_ref, v3_ref, kb_ref, h_ref):
    H, D = DSA_HEADS, DSA_DIM
    j = pl.program_id(1)

    @pl.when(j == 0)
    def _():
        x = x_ref[...]
        ms = jnp.mean(x * x, axis=-1, keepdims=True)
        h_ref[...] = (x * lax.rsqrt(ms + EPS) * g_ref[...]).astype(BF16)

    r = jnp.dot(h_ref[...], w_ref[...], preferred_element_type=F32)

    @pl.when(j == 0)
    def _():
        kb_ref[...] = r.astype(BF16)
        for h in range(H):
            k3_ref[:, h, :] = r[:, h * D:(h + 1) * D]

    @pl.when(j == 1)
    def _():
        for h in range(H):
            v3_ref[:, h, :] = r[:, h * D:(h + 1) * D]


def norm_matmul_heads(x, g, w, tm):
    M, D = x.shape
    H, Dh = DSA_HEADS, DSA_DIM
    return pl.pallas_call(
        _norm_mm_heads_kernel,
        grid=(M // tm, 2),
        in_specs=[pl.BlockSpec((tm, D), lambda i, j: (i, 0)),
                  pl.BlockSpec((1, D), lambda i, j: (0, 0)),
                  pl.BlockSpec((D, H * Dh), lambda i, j: (0, j))],
        out_specs=[pl.BlockSpec((tm, H, Dh), lambda i, j: (i, 0, 0)),
                   pl.BlockSpec((tm, H, Dh), lambda i, j: (i, 0, 0)),
                   pl.BlockSpec((tm, H * Dh), lambda i, j: (i, 0))],
        out_shape=[jax.ShapeDtypeStruct((M, H, Dh), F32), jax.ShapeDtypeStruct((M, H, Dh), F32),
                   jax.ShapeDtypeStruct((M, H * Dh), BF16)],
        scratch_shapes=[pltpu.VMEM((tm, D), BF16)],
        compiler_params=_cparams(("parallel", "arbitrary")),
        name="norm_matmul_heads",
    )(x, g.reshape(1, D), w)


def _norm_mm_t_kernel(x_ref, g_ref, wt_ref, o_ref, h_ref):
    @pl.when(pl.program_id(1) == 0)
    def _():
        x = x_ref[...]
        ms = jnp.mean(x * x, axis=-1, keepdims=True)
        h_ref[...] = (x * lax.rsqrt(ms + EPS) * g_ref[...]).astype(BF16)

    o_ref[...] = lax.dot_general(wt_ref[...], h_ref[...], (((1,), (1,)), ((), ())),
                                 preferred_element_type=F32).astype(o_ref.dtype)


def norm_matmul_t(x, g, wt, tm, tn):
    M, D = x.shape
    N = wt.shape[0]
    return pl.pallas_call(
        _norm_mm_t_kernel,
        grid=(M // tm, N // tn),
        in_specs=[pl.BlockSpec((tm, D), lambda i, j: (i, 0)),
                  pl.BlockSpec((1, D), lambda i, j: (0, 0)),
                  pl.BlockSpec((tn, D), lambda i, j: (j, 0))],
        out_specs=pl.BlockSpec((tn, tm), lambda i, j: (j, i)),
        out_shape=jax.ShapeDtypeStruct((N, M), BF16),
        scratch_shapes=[pltpu.VMEM((tm, D), BF16)],
        compiler_params=_cparams(("parallel", "arbitrary")),
        name="norm_matmul_t",
    )(x, g.reshape(1, D), wt)


def _gla_kernel(q_ref, k_ref, v_ref, g_ref, sm_ref, wg_ref, bg_ref, go_ref, s0_ref,
                o_ref, sfin_ref, st_ref, b_ref, oacc_ref, *, tb, C):
    H, DK, DV = GLA_HEADS, GLA_DK, GLA_DV
    t = pl.program_id(1)

    @pl.when(t == 0)
    def _():
        for h in range(H):
            st_ref[h] = s0_ref[0, h].T

    a_lr = sm_ref[:, SM_GA:SM_GA + GATE_RANK]
    z = jnp.dot(a_lr.astype(BF16), wg_ref[...], preferred_element_type=F32) + bg_ref[...]
    log_a = (jnp.minimum(z, 0.0) - jnp.log1p(jnp.exp(-jnp.abs(z)))) / GATE_TAU
    row = lax.broadcasted_iota(I32, (tb, H * DK), 0) & (C - 1)
    b = log_a
    sh = 1
    while sh < C:
        b = b + jnp.where(row >= sh, pltpu.roll(b, sh, axis=0), 0.0)
        sh *= 2
    b_ref[...] = b

    jj = lax.broadcasted_iota(I32, (C, 1), 0)

    def chunk(c, carry):
        r0 = pl.multiple_of(c * C, C)
        for h in range(H):
            bq = b_ref[pl.ds(r0, C), h * DK:(h + 1) * DK]
            q = q_ref[pl.ds(r0, C), h * DK:(h + 1) * DK] * (DK ** -0.5)
            k = k_ref[pl.ds(r0, C), h * DK:(h + 1) * DK]
            v = v_ref[pl.ds(r0, C), h * DV:(h + 1) * DV]
            st = st_ref[h]
            qe = q * jnp.exp(bq)
            o = lax.dot_general(qe.astype(BF16), st.astype(BF16), (((1,), (1,)), ((), ())),
                                preferred_element_type=F32)
            rows = []
            for i in range(C):
                causal = jj <= i
                d = jnp.where(causal, bq[i:i + 1, :] - bq, 0.0)
                a = jnp.sum(q[i:i + 1, :] * k * jnp.exp(d), axis=-1, keepdims=True)
                a = jnp.where(causal, a, 0.0)
                rows.append(jnp.sum(a * v, axis=0, keepdims=True))
            o = o + jnp.concatenate(rows, axis=0)
            oacc_ref[pl.ds(r0, C), h * DV:(h + 1) * DV] = o
            b_last = bq[C - 1:C, :]
            ke = k * jnp.exp(b_last - bq)
            kv_t = lax.dot_general(v.astype(BF16), ke.astype(BF16), (((0,), (0,)), ((), ())),
                                   preferred_element_type=F32)
            st_ref[h] = st * jnp.exp(b_last) + kv_t
        return carry

    lax.fori_loop(0, tb // C, chunk, 0)

    gate = g_ref[...]
    gate = gate * (1.0 / (1.0 + jnp.exp(-gate)))
    for h in range(H):
        o = oacc_ref[:, h * DV:(h + 1) * DV]
        ms = jnp.mean(o * o, axis=-1, keepdims=True)
        y = o * lax.rsqrt(ms + EPS) * go_ref[...]
        o_ref[:, h * DV:(h + 1) * DV] = (y * gate[:, h * DV:(h + 1) * DV]).astype(o_ref.dtype)

    @pl.when(t == pl.num_programs(1) - 1)
    def _():
        for h in range(H):
            sfin_ref[0, h] = st_ref[h].T


def gla(p, w_gate_up, b_gate, g_gla_out, s0, B, T, tb, C):
    H, DK, DV = GLA_HEADS, GLA_DK, GLA_DV
    nt = T // tb
    row = lambda b, t: b * nt + t
    return pl.pallas_call(
        functools.partial(_gla_kernel, tb=tb, C=C),
        grid=(B, nt),
        in_specs=[pl.BlockSpec((tb, H * DK), lambda b, t: (row(b, t), C_GQ // (H * DK))),
                  pl.BlockSpec((tb, H * DK), lambda b, t: (row(b, t), C_GK // (H * DK))),
                  pl.BlockSpec((tb, H * DV), lambda b, t: (row(b, t), C_GV // (H * DV))),
                  pl.BlockSpec((tb, H * DV), lambda b, t: (row(b, t), C_GG // (H * DV))),
                  pl.BlockSpec((tb, LANES), lambda b, t: (row(b, t), C_SMALL // LANES)),
                  pl.BlockSpec((GATE_RANK, H * DK), lambda b, t: (0, 0)),
                  pl.BlockSpec((1, H * DK), lambda b, t: (0, 0)),
                  pl.BlockSpec((1, DV), lambda b, t: (0, 0)),
                  pl.BlockSpec((1, H, DK, DV), lambda b, t: (b, 0, 0, 0))],
        out_specs=[pl.BlockSpec((tb, H * DV), lambda b, t: (row(b, t), 0)),
                   pl.BlockSpec((1, H, DK, DV), lambda b, t: (b, 0, 0, 0))],
        out_shape=[jax.ShapeDtypeStruct((B * T, H * DV), BF16),
                   jax.ShapeDtypeStruct((B, H, DK, DV), F32)],
        scratch_shapes=[pltpu.VMEM((H, DV, DK), F32),
                        pltpu.VMEM((tb, H * DK), F32),
                        pltpu.VMEM((tb, H * DV), F32)],
        compiler_params=_cparams(("arbitrary", "arbitrary")),
        name="gla",
    )(p, p, p, p, p, w_gate_up.astype(BF16), b_gate.reshape(1, -1), g_gla_out.reshape(1, -1), s0)


def _sortable_key(score):
    bits = lax.bitcast_convert_type(score + 0.0, I32)
    return bits ^ ((bits >> 31) & 0x7FFFFFFF)


def _select_threshold(key_ref, n_chunks, ktarget, pos0, rows, unroll=False):
    ck = key_ref.shape[-1]

    def count(pred):
        def body(c, cnt):
            blk = key_ref[c]
            pos = pos0(c) + lax.broadcasted_iota(I32, (rows, ck), 1)
            m = jnp.where(pred(blk, pos), 1, 0)
            for n in range(ck // LANES):
                cnt = cnt + m[:, n * LANES:(n + 1) * LANES]
            return cnt
        cnt = lax.fori_loop(0, n_chunks, body, jnp.zeros((rows, LANES), I32), unroll=unroll)
        return jnp.sum(cnt, axis=-1, keepdims=True)

    def count_ge(thr):
        return count(lambda blk, pos: blk >= thr)

    thr = jnp.where(count_ge(jnp.zeros((rows, 1), I32)) >= ktarget, 0, INT_MIN).astype(I32)

    def bit_step(it, thr):
        cand = thr | jnp.left_shift(jnp.int32(1), 30 - it)
        return jnp.where(count_ge(cand) >= ktarget, cand, thr)

    thr = lax.fori_loop(0, 31, bit_step, thr)
    n_gt = count(lambda blk, pos: blk > thr)
    n_ge = count_ge(thr)
    need = ktarget - n_gt
    extra = jnp.max(n_ge - n_gt - need)

    def tie_cut():
        def step(_, lohi):
            lo, hi = lohi
            mid = lo + ((hi - lo) >> 1)
            c = count(lambda blk, pos: (blk == thr) & (pos <= mid))
            ok = c >= need
            return jnp.where(ok, lo, mid), jnp.where(ok, mid, hi)
        lo = jnp.full((rows, 1), -1, I32)
        hi = jnp.full((rows, 1), 2 ** 30, I32)
        _, hi = lax.fori_loop(0, 31, step, (lo, hi))
        return hi

    cut = lax.cond(extra > 0, tie_cut, lambda: jnp.full((rows, 1), 2 ** 30, I32))
    return thr, cut


PACK16 = 16
I16_MIN = -32768


def _count16(refs, n_chunks, ck, tq, pred):
    def body(c, cnt):
        r0 = pl.multiple_of(c * ck, ck)
        m = jnp.where(pred(*[r[pl.ds(r0, ck), :] for r in refs], r0), jnp.int16(1), jnp.int16(0))
        for r in range(ck // PACK16):
            cnt = cnt + m[r * PACK16:(r + 1) * PACK16]
        return cnt

    cnt = lax.fori_loop(0, n_chunks, body, jnp.zeros((PACK16, tq), I16))
    return jnp.sum(cnt.astype(I32), axis=0, keepdims=True)


def _kth_largest16(ref, n_chunks, ck, tq, target):
    def count_ge(thr):
        t16 = thr.astype(I16)
        return _count16([ref], n_chunks, ck, tq, lambda blk, r0: blk >= t16)

    thr = jnp.where(count_ge(jnp.zeros((1, tq), I32)) >= target, 0, I16_MIN).astype(I32)

    def bit_step(it, thr):
        cand = thr | jnp.left_shift(jnp.int32(1), 14 - it)
        return jnp.where(count_ge(cand) >= target, cand, thr)

    return lax.fori_loop(0, 15, bit_step, thr)


def _idx_kernel(iqt_ref, wt_ref, ki_ref, bias_ref, hi_ref, lo_ref, *, tq, ck, k_sel):
    i = pl.program_id(0)
    T = hi_ref.shape[0]
    n_all = T // ck
    n_chunks = ((i + 1) * tq + ck - 1) // ck
    t_pos = i * tq + lax.broadcasted_iota(I32, (1, tq), 1)
    w = wt_ref[...]

    def score_chunk(c, carry):
        r0 = pl.multiple_of(c * ck, ck)
        kc = ki_ref[pl.ds(r0, ck), :]
        acc = jnp.zeros((ck, tq), F32)
        for h in range(IDX_HEADS):
            d = jnp.dot(kc, iqt_ref[h * IDX_DIM:(h + 1) * IDX_DIM, :], preferred_element_type=F32)
            acc = acc + jnp.maximum(d, 0.0) * w[h:h + 1, :]
        s_pos = r0 + lax.broadcasted_iota(I32, (ck, tq), 0)
        key = _sortable_key(jnp.where(s_pos <= t_pos, acc, -jnp.inf))
        hi_ref[pl.ds(r0, ck), :] = (key >> 16).astype(I16)
        lo_ref[pl.ds(r0, ck), :] = ((key & 0xFFFF) + I16_MIN).astype(I16)
        return carry

    lax.fori_loop(0, n_chunks, score_chunk, 0)

    ktarget = jnp.minimum(t_pos + 1, k_sel)
    thr_hi = _kth_largest16(hi_ref, n_chunks, ck, tq, ktarget)
    th16 = thr_hi.astype(I16)
    n_gt_hi = _count16([hi_ref], n_chunks, ck, tq, lambda blk, r0: blk > th16)
    r_need = ktarget - n_gt_hi

    def mask_low(c, carry):
        r0 = pl.multiple_of(c * ck, ck)
        lo_ref[pl.ds(r0, ck), :] = jnp.where(hi_ref[pl.ds(r0, ck), :] == th16, lo_ref[pl.ds(r0, ck), :],
                                             jnp.int16(I16_MIN))
        return carry

    lax.fori_loop(0, n_chunks, mask_low, 0)

    thr_lo = _kth_largest16(lo_ref, n_chunks, ck, tq, r_need)
    tl16 = thr_lo.astype(I16)
    n_gt_lo = _count16([lo_ref], n_chunks, ck, tq, lambda blk, r0: blk > tl16)
    one, zero = jnp.int16(1), jnp.int16(0)

    def is_tie(h, l):
        return jnp.where(h == th16, jnp.where(l == tl16, one, zero), zero)

    n_tie = _count16([hi_ref, lo_ref], n_chunks, ck, tq, lambda h, l, r0: is_tie(h, l) > zero)
    need = r_need - n_gt_lo
    extra = jnp.max(n_tie - need)

    def pos16(r0):
        return (r0 + lax.broadcasted_iota(I32, (ck, tq), 0)).astype(I16)

    def write_bias(sel_fn):
        def body(c, carry):
            r0 = pl.multiple_of(c * ck, ck)
            sel = sel_fn(hi_ref[pl.ds(r0, ck), :], lo_ref[pl.ds(r0, ck), :], r0)
            bias_ref[0, pl.ds(r0, ck), :] = jnp.where(sel > zero, jnp.asarray(0.0, BF16), jnp.asarray(MASKED, BF16))
            return carry
        lax.fori_loop(0, n_chunks, body, 0)

    def no_surplus_ties():
        write_bias(lambda h, l, r0: jnp.where(h > th16, one, jnp.where(h == th16, jnp.where(l >= tl16, one, zero), zero)))

    def surplus_ties():
        def step(_, lohi):
            lo, hi = lohi
            mid = lo + ((hi - lo) >> 1)
            m16 = mid.astype(I16)
            c = _count16([hi_ref, lo_ref], n_chunks, ck, tq,
                         lambda h, l, r0: jnp.where(pos16(r0) <= m16, is_tie(h, l), zero) > zero)
            ok = c >= need
            return jnp.where(ok, lo, mid), jnp.where(ok, mid, hi)

        _, cut = lax.fori_loop(0, 15, step, (jnp.full((1, tq), -1, I32), jnp.full((1, tq), T - 1, I32)))
        c16 = cut.astype(I16)

        def sel_fn(h, l, r0):
            tie_ok = jnp.where(pos16(r0) <= c16, one, zero)
            in_lo = jnp.where(l > tl16, one, jnp.where(l == tl16, tie_ok, zero))
            return jnp.where(h > th16, one, jnp.where(h == th16, in_lo, zero))
        write_bias(sel_fn)

    lax.cond(extra > 0, surplus_ties, no_surplus_ties)

    def write_rest(c, carry):
        r0 = pl.multiple_of(c * ck, ck)
        bias_ref[0, pl.ds(r0, ck), :] = jnp.full((ck, tq), MASKED, bias_ref.dtype)
        return carry

    lax.fori_loop(n_chunks, n_all, write_rest, 0)


def idx_select(iqt, wt, ki, T, tq, ck):
    k_sel = min(TOPK_MAX, T // 4)
    return pl.pallas_call(
        functools.partial(_idx_kernel, tq=tq, ck=ck, k_sel=k_sel),
        grid=(T // tq,),
        in_specs=[pl.BlockSpec((IDX_HEADS * IDX_DIM, tq), lambda i: (0, i)),
                  pl.BlockSpec((IDX_HEADS, tq), lambda i: (0, i)),
                  pl.BlockSpec((T, IDX_DIM), lambda i: (0, 0))],
        out_specs=pl.BlockSpec((1, T, tq), lambda i: (i, 0, 0)),
        out_shape=jax.ShapeDtypeStruct((T // tq, T, tq), BF16),
        scratch_shapes=[pltpu.VMEM((T, tq), I16), pltpu.VMEM((T, tq), I16)],
        compiler_params=pltpu.CompilerParams(dimension_semantics=("arbitrary",), vmem_limit_bytes=IDX_VMEM_LIMIT),
        name="idx_select",
    )(iqt, wt, ki)


DSA_DIM_AUG = DSA_DIM + 16
LOG2E = 1.4426950408889634


def _attn_kernel(ii_ref, jj_ref, qt_ref, k_ref, vt_ref, b_ref, o_ref, m_ref, l_ref, acc_ref, *, tq, tk):
    H, D, DA = DSA_HEADS, DSA_DIM, DSA_DIM_AUG
    s_id = pl.program_id(0)
    i, j = ii_ref[s_id], jj_ref[s_id]

    @pl.when(j == 0)
    def _():
        m_ref[...] = jnp.full(m_ref.shape, M_INIT, F32)
        l_ref[...] = jnp.zeros(l_ref.shape, F32)
        acc_ref[...] = jnp.zeros(acc_ref.shape, F32)

    bias = jnp.concatenate([b_ref[n] for n in range(b_ref.shape[0])], axis=1).astype(F32)

    def qk(h):
        return jnp.dot(k_ref[:, h * D:(h + 1) * D], qt_ref[h * D:(h + 1) * D, :], preferred_element_type=F32)

    s_next = qk(0)
    for h in range(H):
        s = s_next
        if h + 1 < H:
            s_next = qk(h + 1)
        s = s * (D ** -0.5 * LOG2E) + bias
        m_prev = m_ref[h:h + 1, :]
        m_new = jnp.maximum(m_prev, jnp.max(s, axis=0, keepdims=True))
        alpha = jnp.exp2(m_prev - m_new)
        pr = jnp.exp2(s - m_new).astype(BF16)
        pv = jnp.dot(vt_ref[h * DA:(h + 1) * DA, :], pr, preferred_element_type=F32)
        acc_ref[h * D:(h + 1) * D, :] = alpha * acc_ref[h * D:(h + 1) * D, :] + pv[:D]
        l_ref[h:h + 1, :] = alpha * l_ref[h:h + 1, :] + pv[D:D + 1]
        m_ref[h:h + 1, :] = m_new

    @pl.when(j == (i * tq + tq - 1) // tk)
    def _():
        for h in range(H):
            o_ref[h * D:(h + 1) * D, :] = (acc_ref[h * D:(h + 1) * D, :] / l_ref[h:h + 1, :]).astype(o_ref.dtype)


def attention(qt, k, vt_aug, bias, T, tq, tk):
    H, D, DA = DSA_HEADS, DSA_DIM, DSA_DIM_AUG
    tqb = bias.shape[2]
    pairs = [(i, j) for i in range(T // tq) for j in range((i * tq + tq - 1) // tk + 1)]
    ii = jnp.asarray([p[0] for p in pairs], I32)
    jj = jnp.asarray([p[1] for p in pairs], I32)
    grid_spec = pltpu.PrefetchScalarGridSpec(
        num_scalar_prefetch=2,
        grid=(len(pairs),),
        in_specs=[pl.BlockSpec((H * D, tq), lambda s, ii, jj: (0, ii[s])),
                  pl.BlockSpec((tk, H * D), lambda s, ii, jj: (jj[s], 0)),
                  pl.BlockSpec((H * DA, tk), lambda s, ii, jj: (0, jj[s])),
                  pl.BlockSpec((tq // tqb, tk, tqb), lambda s, ii, jj: (ii[s], jj[s], 0))],
        out_specs=pl.BlockSpec((H * D, tq), lambda s, ii, jj: (0, ii[s])),
        scratch_shapes=[pltpu.VMEM((H, tq), F32), pltpu.VMEM((H, tq), F32), pltpu.VMEM((H * D, tq), F32)],
    )
    return pl.pallas_call(
        functools.partial(_attn_kernel, tq=tq, tk=tk),
        grid_spec=grid_spec,
        out_shape=jax.ShapeDtypeStruct((H * D, T), BF16),
        compiler_params=_cparams(("arbitrary",)),
        name="attention",
    )(ii, jj, qt, k, vt_aug, bias)


def _idx_sample_kernel(pt_ref, q_ref, w_ref, kn_ref, cache_ref, bias_ref, kbuf_ref, key_ref, sem, *,
                       n_pages, ck, k_sel, t_dec):
    b = pl.program_id(0)
    n_chunks = key_ref.shape[0]

    def page_copy(pg):
        return pltpu.make_async_copy(cache_ref.at[pt_ref[b, pg]], kbuf_ref.at[pl.ds(pg * PAGE, PAGE)], sem)

    def start(pg, c):
        page_copy(pg).start()
        return c

    lax.fori_loop(0, n_pages, start, 0)
    kbuf_ref[pl.ds(n_pages * PAGE, ck)] = jnp.concatenate(
        [kn_ref[0], jnp.zeros((ck - t_dec, IDX_DIM), F32)], axis=0)

    def wait(pg, c):
        page_copy(pg).wait()
        return c

    lax.fori_loop(0, n_pages, wait, 0)

    q = q_ref[0]
    wcol = w_ref[0]
    t_pos = n_pages * PAGE + lax.broadcasted_iota(I32, (t_dec, 1), 0)

    def score_chunk(c, carry):
        kc = kbuf_ref[pl.ds(pl.multiple_of(c * ck, ck), ck)].astype(BF16)
        d = lax.dot_general(q, kc, (((1,), (1,)), ((), ())), preferred_element_type=F32)
        d = jnp.maximum(d, 0.0) * wcol
        sc = d[0:t_dec]
        for h in range(1, IDX_HEADS):
            sc = sc + d[h * t_dec:(h + 1) * t_dec]
        s_pos = c * ck + lax.broadcasted_iota(I32, (t_dec, ck), 1)
        key_ref[c] = _sortable_key(jnp.where(s_pos <= t_pos, sc, -jnp.inf))
        return carry

    lax.fori_loop(0, n_chunks, score_chunk, 0)

    ktarget = jnp.full((t_dec, 1), k_sel, I32)
    thr, cut = _select_threshold(key_ref, n_chunks, ktarget, lambda c: c * ck, t_dec, unroll=True)

    def write_sel(c, carry):
        blk = key_ref[c]
        pos = c * ck + lax.broadcasted_iota(I32, (t_dec, ck), 1)
        sel = (blk > thr) | ((blk == thr) & (pos <= cut))
        bias_ref[0, c] = jnp.where(sel, 0.0, MASKED)
        return carry

    lax.fori_loop(0, n_chunks, write_sel, 0)


def idx_select_sample(iq, iw, kn, cache_idx_k, page_table, B, t_dec, ck):
    n_pages = page_table.shape[1]
    past = n_pages * PAGE
    k_sel = min(TOPK_MAX, (past + t_dec) // 4)
    n_chunks = past // ck + 1
    iw = iw.reshape(B, t_dec, IDX_HEADS)
    iw = (iw * (IDX_HEADS ** -0.5 * IDX_DIM ** -0.5)).transpose(0, 2, 1).reshape(B, IDX_HEADS * t_dec, 1)
    kn = kn.reshape(B, t_dec, IDX_DIM)
    grid_spec = pltpu.PrefetchScalarGridSpec(
        num_scalar_prefetch=1,
        grid=(B,),
        in_specs=[pl.BlockSpec((1, IDX_HEADS * t_dec, IDX_DIM), lambda b, pt: (b, 0, 0)),
                  pl.BlockSpec((1, IDX_HEADS * t_dec, 1), lambda b, pt: (b, 0, 0)),
                  pl.BlockSpec((1, t_dec, IDX_DIM), lambda b, pt: (b, 0, 0)),
                  pl.BlockSpec(memory_space=pl.ANY)],
        out_specs=pl.BlockSpec((1, n_chunks, t_dec, ck), lambda b, pt: (b, 0, 0, 0)),
        scratch_shapes=[pltpu.VMEM((past + ck, IDX_DIM), F32),
                        pltpu.VMEM((n_chunks, t_dec, ck), I32),
                        pltpu.SemaphoreType.DMA(())],
    )
    return pl.pallas_call(
        functools.partial(_idx_sample_kernel, n_pages=n_pages, ck=ck, k_sel=k_sel, t_dec=t_dec),
        grid_spec=grid_spec,
        out_shape=jax.ShapeDtypeStruct((B, n_chunks, t_dec, ck), F32),
        compiler_params=_cparams(("arbitrary",)),
        name="idx_select_sample",
    )(page_table, iq, iw, kn, cache_idx_k)


def _attn_sample_kernel(pt_ref, q_ref, kn_ref, vn_ref, b_ref, hm_ref, *refs, pg, t_dec):
    H, D = DSA_HEADS, DSA_DIM
    k_refs, v_refs = refs[:pg], refs[pg:2 * pg]
    o_ref, m_ref, l_ref, acc_ref = refs[2 * pg:]
    j = pl.program_id(1)
    nj = pl.num_programs(1)

    @pl.when(j == 0)
    def _():
        m_ref[...] = jnp.full(m_ref.shape, M_INIT, F32)
        l_ref[...] = jnp.zeros(l_ref.shape, F32)
        acc_ref[...] = jnp.zeros(acc_ref.shape, F32)

    def update(kc, vc):
        bias = b_ref[0, 0].astype(F32)
        s = lax.dot_general(q_ref[0], kc, (((1,), (1,)), ((), ())), preferred_element_type=F32)
        s = s * (D ** -0.5) + jnp.concatenate([bias] * H, axis=0) + hm_ref[...]
        m_prev = m_ref[...]
        m_new = jnp.maximum(m_prev, jnp.max(s, axis=-1, keepdims=True))
        alpha = jnp.exp(m_prev - m_new)
        pr = jnp.exp(s - m_new)
        l_ref[...] = alpha * l_ref[...] + jnp.sum(pr, axis=-1, keepdims=True)
        acc_ref[...] = alpha * acc_ref[...] + jnp.dot(pr.astype(BF16), vc, preferred_element_type=F32)
        m_ref[...] = m_new

    @pl.when(j < nj - 1)
    def _():
        kc = jnp.concatenate([r[0] for r in k_refs], axis=0).astype(BF16)
        vc = jnp.concatenate([r[0] for r in v_refs], axis=0).astype(BF16)
        update(kc, vc)

    @pl.when(j == nj - 1)
    def _():
        pad = jnp.zeros(((pg * PAGE - t_dec) * H, D), F32)
        kc = jnp.concatenate([kn_ref[0], pad], axis=0).astype(BF16)
        vc = jnp.concatenate([vn_ref[0], pad], axis=0).astype(BF16)
        update(kc, vc)
        out = acc_ref[...] / l_ref[...]
        for h in range(H):
            o_ref[0, :, h * D:(h + 1) * D] = out[h * t_dec:(h + 1) * t_dec].astype(o_ref.dtype)


def attention_sample(q, kn, vn, cache_k, cache_v, page_table, bias, B, t_dec, pg):
    H, D = DSA_HEADS, DSA_DIM
    n_pages = page_table.shape[1]
    nj = n_pages // pg + 1
    nk = pg * PAGE * H
    q_head = jnp.arange(H * t_dec, dtype=I32)[:, None] // t_dec
    k_head = jnp.arange(nk, dtype=I32)[None, :] % H
    head_mask = jnp.where(q_head == k_head, 0.0, MASKED).astype(F32)

    def page_map(u):
        return lambda b, j, pt: (pt[b, jnp.minimum(j, n_pages // pg - 1) * pg + u], 0, 0)

    page_specs = [pl.BlockSpec((1, PAGE * H, D), page_map(u)) for u in range(pg)]
    grid_spec = pltpu.PrefetchScalarGridSpec(
        num_scalar_prefetch=1,
        grid=(B, nj),
        in_specs=[pl.BlockSpec((1, H * t_dec, D), lambda b, j, pt: (b, 0, 0)),
                  pl.BlockSpec((1, t_dec * H, D), lambda b, j, pt: (b, 0, 0)),
                  pl.BlockSpec((1, t_dec * H, D), lambda b, j, pt: (b, 0, 0)),
                  pl.BlockSpec((1, 1, t_dec, nk), lambda b, j, pt: (b, j, 0, 0)),
                  pl.BlockSpec((H * t_dec, nk), lambda b, j, pt: (0, 0))]
        + page_specs + page_specs,
        out_specs=pl.BlockSpec((1, t_dec, H * D), lambda b, j, pt: (b, 0, 0)),
        scratch_shapes=[pltpu.VMEM((H * t_dec, 1), F32), pltpu.VMEM((H * t_dec, 1), F32),
                        pltpu.VMEM((H * t_dec, D), F32)],
    )
    return pl.pallas_call(
        functools.partial(_attn_sample_kernel, pg=pg, t_dec=t_dec),
        grid_spec=grid_spec,
        out_shape=jax.ShapeDtypeStruct((B, t_dec, H * D), F32),
        compiler_params=_cparams(("parallel", "arbitrary")),
        name="attention_sample",
    )(page_table, q, kn, vn, bias, head_mask, *([cache_k] * pg), *([cache_v] * pg))


def _out_proj_kernel(x_ref, a_ref, b_ref, wa_ref, wb_ref, o_ref):
    o_ref[...] = (x_ref[...] + jnp.dot(a_ref[...], wa_ref[...], preferred_element_type=F32)
                  + jnp.dot(b_ref[...], wb_ref[...], preferred_element_type=F32))


def out_proj(x, a, b, wa, wb, tm, tn):
    M, D = x.shape
    Ka, Kb = a.shape[1], b.shape[1]
    return pl.pallas_call(
        _out_proj_kernel,
        grid=(M // tm, D // tn),
        in_specs=[pl.BlockSpec((tm, tn), lambda i, j: (i, j)),
                  pl.BlockSpec((tm, Ka), lambda i, j: (i, 0)),
                  pl.BlockSpec((tm, Kb), lambda i, j: (i, 0)),
                  pl.BlockSpec((Ka, tn), lambda i, j: (0, j)),
                  pl.BlockSpec((Kb, tn), lambda i, j: (0, j))],
        out_specs=pl.BlockSpec((tm, tn), lambda i, j: (i, j)),
        out_shape=jax.ShapeDtypeStruct((M, D), F32),
        compiler_params=_cparams(("parallel", "arbitrary")),
        name="out_proj",
    )(x, a, b, wa, wb)


def _ffn_kernel(x_ref, g_ref, wu_ref, wd_ref, gf_ref, o_ref, h_ref, acc_ref):
    f = pl.program_id(1)

    @pl.when(f == 0)
    def _():
        x = x_ref[...]
        ms = jnp.mean(x * x, axis=-1, keepdims=True)
        h_ref[...] = (x * lax.rsqrt(ms + EPS) * g_ref[...]).astype(BF16)
        acc_ref[...] = jnp.zeros(acc_ref.shape, F32)

    u = jnp.dot(h_ref[...], wu_ref[...], preferred_element_type=F32)
    u = jnp.square(jnp.maximum(u, 0.0)).astype(BF16)
    acc_ref[...] += jnp.dot(u, wd_ref[...], preferred_element_type=F32)

    @pl.when(f == pl.num_programs(1) - 1)
    def _():
        x2 = x_ref[...] + acc_ref[...]
        ms = jnp.mean(x2 * x2, axis=-1, keepdims=True)
        o_ref[...] = x2 * lax.rsqrt(ms + EPS) * gf_ref[...]


def ffn(x, g_ffn, w_up, w_down, g_final, tm, tf):
    M, D = x.shape
    Fd = w_up.shape[1]
    return pl.pallas_call(
        _ffn_kernel,
        grid=(M // tm, Fd // tf),
        in_specs=[pl.BlockSpec((tm, D), lambda i, f: (i, 0)),
                  pl.BlockSpec((1, D), lambda i, f: (0, 0)),
                  pl.BlockSpec((D, tf), lambda i, f: (0, f)),
                  pl.BlockSpec((tf, D), lambda i, f: (f, 0)),
                  pl.BlockSpec((1, D), lambda i, f: (0, 0))],
        out_specs=pl.BlockSpec((tm, D), lambda i, f: (i, 0)),
        out_shape=jax.ShapeDtypeStruct((M, D), F32),
        scratch_shapes=[pltpu.VMEM((tm, D), BF16), pltpu.VMEM((tm, D), F32)],
        compiler_params=_cparams(("parallel", "arbitrary")),
        name="ffn",
    )(x, g_ffn.reshape(1, D), w_up, w_down, g_final.reshape(1, D))


def _pack_w_in(w_in):
    D = w_in.shape[0]
    gq, gk, gv, gg, ga, aq, ak, av, iq, iw, ik = jnp.split(
        w_in, [512, 1024, 2048, 3072, 3088, 4112, 5136, 6160, 7184, 7200], axis=1)
    pad = jnp.zeros((D, PA_WIDTH - C_SMALL - 96), w_in.dtype)
    w_a = jnp.concatenate([gq, gk, gv, gg, ik, ga, iw, pad], axis=1).astype(BF16)
    w_b = jnp.concatenate([ak, av], axis=1).astype(BF16)
    w_ct = jnp.concatenate([aq, av, iq], axis=1).T.astype(BF16)
    return w_a, w_b, w_ct


def _tile(n, pref):
    return pref if n % pref == 0 else n


def kernel(x_prompt, x_sample, cache_k, cache_v, cache_idx_k, state_gla, page_table,
           g_mix, w_in, w_gate_up, b_gate, g_gla_out, w_o, g_ffn, w_up, w_down, g_final):
    depth = w_in.shape[0]
    Bp, Tp, D = x_prompt.shape
    Bs, Ts, _ = x_sample.shape
    assert Bp == 1 and depth == 1
    H, Dh = DSA_HEADS, DSA_DIM
    n_pool = cache_k.shape[1]

    xp = x_prompt.reshape(Bp * Tp, D)
    xs = x_sample.reshape(Bs * Ts, D)
    l = 0
    w_a, w_b, w_ct = _pack_w_in(w_in[l])
    wo_a = w_o[l, :GLA_HEADS * GLA_DV].astype(BF16)
    wo_b = w_o[l, GLA_HEADS * GLA_DV:].astype(BF16)
    wu = w_up[l].astype(BF16)
    wd = w_down[l].astype(BF16)
    idx_scale = IDX_HEADS ** -0.5 * IDX_DIM ** -0.5

    tm = _tile(Tp, 512)
    pa_p = norm_matmul(xp, g_mix[l], w_a, tm, PA_WIDTH // 2)
    k3_p, v3_p, kb_p = norm_matmul_heads(xp, g_mix[l], w_b, tm)
    ct_p = norm_matmul_t(xp, g_mix[l], w_ct, tm, 1024)
    s0 = jnp.zeros((Bp, GLA_HEADS, GLA_DK, GLA_DV), F32)
    o_gla_p, s_fin_p = gla(pa_p, w_gate_up[l], b_gate[l], g_gla_out[l], s0, Bp, Tp, _tile(Tp, 128), 16)
    ik_p = pa_p[:, C_SMALL + SM_IK:C_SMALL + SM_IK + IDX_DIM]
    wt = (pa_p[:, C_SMALL + SM_IW:C_SMALL + SM_IW + IDX_HEADS] * idx_scale).T
    bias = idx_select(ct_p[CT_IQ:CT_IQ + IDX_HEADS * IDX_DIM], wt, ik_p.astype(BF16), Tp, _tile(Tp, 256), _tile(Tp, 512))
    vt = ct_p[CT_AV:CT_AV + H * Dh].reshape(H, Dh, Tp)
    vt_aug = jnp.concatenate([vt, jnp.ones((H, DSA_DIM_AUG - Dh, Tp), BF16)], axis=1).reshape(H * DSA_DIM_AUG, Tp)
    o_dsa_p = attention(ct_p[CT_AQ:CT_AQ + H * Dh], kb_p, vt_aug, bias, Tp, tm, tm).T
    x1p = out_proj(xp, o_gla_p, o_dsa_p, wo_a, wo_b, tm, 1024)
    y_p = ffn(x1p, g_ffn[l], wu, wd, g_final, tm, 512)

    Ms = Bs * Ts
    pa_s = norm_matmul(xs, g_mix[l], w_a, Ms, PA_WIDTH // 2)
    k3_s, v3_s, _ = norm_matmul_heads(xs, g_mix[l], w_b, Ms)
    ct_s = norm_matmul_t(xs, g_mix[l], w_ct, Ms, 1024)
    o_gla_s, s_fin_s = gla(pa_s, w_gate_up[l], b_gate[l], g_gla_out[l], state_gla[l], Bs, Ts, Ts, Ts)
    ik_s = pa_s[:, C_SMALL + SM_IK:C_SMALL + SM_IK + IDX_DIM]
    iq_s = ct_s[CT_IQ:CT_IQ + IDX_HEADS * IDX_DIM].reshape(IDX_HEADS, IDX_DIM, Bs, Ts)
    iq_s = iq_s.transpose(2, 0, 3, 1).reshape(Bs, IDX_HEADS * Ts, IDX_DIM)
    pg = 4
    bias_s = idx_select_sample(iq_s, pa_s[:, C_SMALL + SM_IW:C_SMALL + SM_IW + IDX_HEADS], ik_s,
                               cache_idx_k[l], page_table, Bs, Ts, pg * PAGE)
    bias_s = jnp.repeat(bias_s.astype(BF16), H, axis=-1)
    q_s = ct_s[CT_AQ:CT_AQ + H * Dh].reshape(H, Dh, Bs, Ts).transpose(2, 0, 3, 1).reshape(Bs, H * Ts, Dh)
    o_dsa_s = attention_sample(q_s, k3_s.reshape(Bs, Ts * H, Dh), v3_s.reshape(Bs, Ts * H, Dh),
                               cache_k[l].reshape(n_pool, PAGE * H, Dh), cache_v[l].reshape(n_pool, PAGE * H, Dh),
                               page_table, bias_s, Bs, Ts, pg)
    x1s = out_proj(xs, o_gla_s, o_dsa_s.reshape(Ms, H * Dh).astype(BF16), wo_a, wo_b, Ms, 1024)
    y_s = ffn(x1s, g_ffn[l], wu, wd, g_final, Ms, 512)

    return (y_p.reshape(Bp, Tp, D), y_s.reshape(Bs, Ts, D),
            k3_p.reshape(1, Bp, Tp, H, Dh), v3_p.reshape(1, Bp, Tp, H, Dh),
            ik_p.reshape(1, Bp, Tp, IDX_DIM), s_fin_p[None],
            k3_s.reshape(1, Bs, Ts, H, Dh), v3_s.reshape(1, Bs, Ts, H, Dh),
            ik_s.reshape(1, Bs, Ts, IDX_DIM), s_fin_s[None])
```

```python
import functools

import jax
import jax.numpy as jnp
from jax import lax
from jax.experimental import pallas as pl
from jax.experimental.pallas import tpu as pltpu

F32 = jnp.float32
BF16 = jnp.bfloat16
I32 = jnp.int32
I16 = jnp.int16

EPS = 1e-6
GATE_TAU = 16.0
GLA_HEADS = 4
GLA_DK = 128
GLA_DV = 256
GATE_RANK = 16
DSA_HEADS = 8
DSA_DIM = 128
IDX_HEADS = 16
IDX_DIM = 64
TOPK_MAX = 256
PAGE = 128

LANES = 128
MASKED = -1e30
M_INIT = -1e29
INT_MIN = -(2 ** 31)
VMEM_LIMIT = 52 * 1024 * 1024
IDX_VMEM_LIMIT = 56 * 1024 * 1024

C_GQ, C_GK, C_GV, C_GG = 0, 512, 1024, 2048
C_SMALL = 3072
SM_IK, SM_GA, SM_IW = 0, 64, 80
PA_WIDTH = 3328
CT_AQ, CT_AV, CT_IQ = 0, 1024, 2048
CT_ROWS = 3072


def _cparams(sem):
    return pltpu.CompilerParams(dimension_semantics=sem, vmem_limit_bytes=VMEM_LIMIT)


def _norm_mm_kernel(x_ref, g_ref, w_ref, o_ref, h_ref):
    @pl.when(pl.program_id(1) == 0)
    def _():
        x = x_ref[...]
        ms = jnp.mean(x * x, axis=-1, keepdims=True)
        h_ref[...] = (x * lax.rsqrt(ms + EPS) * g_ref[...]).astype(BF16)

    o_ref[...] = jnp.dot(h_ref[...], w_ref[...], preferred_element_type=F32)


def norm_matmul(x, g, w, tm, tn):
    M, D = x.shape
    N = w.shape[1]
    return pl.pallas_call(
        _norm_mm_kernel,
        grid=(M // tm, N // tn),
        in_specs=[pl.BlockSpec((tm, D), lambda i, j: (i, 0)),
                  pl.BlockSpec((1, D), lambda i, j: (0, 0)),
                  pl.BlockSpec((D, tn), lambda i, j: (0, j))],
        out_specs=pl.BlockSpec((tm, tn), lambda i, j: (i, j)),
        out_shape=jax.ShapeDtypeStruct((M, N), F32),
        scratch_shapes=[pltpu.VMEM((tm, D), BF16)],
        compiler_params=_cparams(("parallel", "arbitrary")),
        name="norm_matmul",
    )(x, g.reshape(1, D), w)


def _norm_mm_heads_kernel(x_ref, g_ref, w_ref, k3_ref, v3_ref, kb_ref, h_ref):
    H, D = DSA_HEADS, DSA_DIM
    j = pl.program_id(1)

    @pl.when(j == 0)
    def _():
        x = x_ref[...]
        ms = jnp.mean(x * x, axis=-1, keepdims=True)
        h_ref[...] = (x * lax.rsqrt(ms + EPS) * g_ref[...]).astype(BF16)

    r = jnp.dot(h_ref[...], w_ref[...], preferred_element_type=F32)

    @pl.when(j == 0)
    def _():
        kb_ref[...] = r.astype(BF16)
        for h in range(H):
            k3_ref[:, h, :] = r[:, h * D:(h + 1) * D]

    @pl.when(j == 1)
    def _():
        for h in range(H):
            v3_ref[:, h, :] = r[:, h * D:(h + 1) * D]


def norm_matmul_heads(x, g, w, tm):
    M, D = x.shape
    H, Dh = DSA_HEADS, DSA_DIM
    return pl.pallas_call(
        _norm_mm_heads_kernel,
        grid=(M // tm, 2),
        in_specs=[pl.BlockSpec((tm, D), lambda i, j: (i, 0)),
                  pl.BlockSpec((1, D), lambda i, j: (0, 0)),
                  pl.BlockSpec((D, H * Dh), lambda i, j: (0, j))],
        out_specs=[pl.BlockSpec((tm, H, Dh), lambda i, j: (i, 0, 0)),
                   pl.BlockSpec((tm, H, Dh), lambda i, j: (i, 0, 0)),
                   pl.BlockSpec((tm, H * Dh), lambda i, j: (i, 0))],
        out_shape=[jax.ShapeDtypeStruct((M, H, Dh), F32), jax.ShapeDtypeStruct((M, H, Dh), F32),
                   jax.ShapeDtypeStruct((M, H * Dh), BF16)],
        scratch_shapes=[pltpu.VMEM((tm, D), BF16)],
        compiler_params=_cparams(("parallel", "arbitrary")),
        name="norm_matmul_heads",
    )(x, g.reshape(1, D), w)


def _norm_mm_t_kernel(x_ref, g_ref, wt_ref, o_ref, h_ref):
    @pl.when(pl.program_id(1) == 0)
    def _():
        x = x_ref[...]
        ms = jnp.mean(x * x, axis=-1, keepdims=True)
        h_ref[...] = (x * lax.rsqrt(ms + EPS) * g_ref[...]).astype(BF16)

    o_ref[...] = lax.dot_general(wt_ref[...], h_ref[...], (((1,), (1,)), ((), ())),
                                 preferred_element_type=F32).astype(o_ref.dtype)


def norm_matmul_t(x, g, wt, tm, tn):
    M, D = x.shape
    N = wt.shape[0]
    return pl.pallas_call(
        _norm_mm_t_kernel,
        grid=(M // tm, N // tn),
        in_specs=[pl.BlockSpec((tm, D), lambda i, j: (i, 0)),
                  pl.BlockSpec((1, D), lambda i, j: (0, 0)),
                  pl.BlockSpec((tn, D), lambda i, j: (j, 0))],
        out_specs=pl.BlockSpec((tn, tm), lambda i, j: (j, i)),
        out_shape=jax.ShapeDtypeStruct((N, M), BF16),
        scratch_shapes=[pltpu.VMEM((tm, D), BF16)],
        compiler_params=_cparams(("parallel", "arbitrary")),
        name="norm_matmul_t",
    )(x, g.reshape(1, D), wt)


def _gla_kernel(q_ref, k_ref, v_ref, g_ref, sm_ref, wg_ref, bg_ref, go_ref, s0_ref,
                o_ref, sfin_ref, st_ref, b_ref, oacc_ref, *, tb, C):
    H, DK, DV = GLA_HEADS, GLA_DK, GLA_DV
    t = pl.program_id(1)

    @pl.when(t == 0)
    def _():
        for h in range(H):
            st_ref[h] = s0_ref[0, h].T

    a_lr = sm_ref[:, SM_GA:SM_GA + GATE_RANK]
    z = jnp.dot(a_lr.astype(BF16), wg_ref[...], preferred_element_type=F32) + bg_ref[...]
    log_a = (jnp.minimum(z, 0.0) - jnp.log1p(jnp.exp(-jnp.abs(z)))) / GATE_TAU
    row = lax.broadcasted_iota(I32, (tb, H * DK), 0) & (C - 1)
    b = log_a
    sh = 1
    while sh < C:
        b = b + jnp.where(row >= sh, pltpu.roll(b, sh, axis=0), 0.0)
        sh *= 2
    b_ref[...] = b

    jj = lax.broadcasted_iota(I32, (C, 1), 0)

    def chunk(c, carry):
        r0 = pl.multiple_of(c * C, C)
        for h in range(H):
            bq = b_ref[pl.ds(r0, C), h * DK:(h + 1) * DK]
            q = q_ref[pl.ds(r0, C), h * DK:(h + 1) * DK] * (DK ** -0.5)
            k = k_ref[pl.ds(r0, C), h * DK:(h + 1) * DK]
            v = v_ref[pl.ds(r0, C), h * DV:(h + 1) * DV]
            st = st_ref[h]
            qe = q * jnp.exp(bq)
            o = lax.dot_general(qe.astype(BF16), st.astype(BF16), (((1,), (1,)), ((), ())),
                                preferred_element_type=F32)
            rows = []
            for i in range(C):
                causal = jj <= i
                d = jnp.where(causal, bq[i:i + 1, :] - bq, 0.0)
                a = jnp.sum(q[i:i + 1, :] * k * jnp.exp(d), axis=-1, keepdims=True)
                a = jnp.where(causal, a, 0.0)
                rows.append(jnp.sum(a * v, axis=0, keepdims=True))
            o = o + jnp.concatenate(rows, axis=0)
            oacc_ref[pl.ds(r0, C), h * DV:(h + 1) * DV] = o
            b_last = bq[C - 1:C, :]
            ke = k * jnp.exp(b_last - bq)
            kv_t = lax.dot_general(v.astype(BF16), ke.astype(BF16), (((0,), (0,)), ((), ())),
                                   preferred_element_type=F32)
            st_ref[h] = st * jnp.exp(b_last) + kv_t
        return carry

    lax.fori_loop(0, tb // C, chunk, 0)

    gate = g_ref[...]
    gate = gate * (1.0 / (1.0 + jnp.exp(-gate)))
    for h in range(H):
        o = oacc_ref[:, h * DV:(h + 1) * DV]
        ms = jnp.mean(o * o, axis=-1, keepdims=True)
        y = o * lax.rsqrt(ms + EPS) * go_ref[...]
        o_ref[:, h * DV:(h + 1) * DV] = (y * gate[:, h * DV:(h + 1) * DV]).astype(o_ref.dtype)

    @pl.when(t == pl.num_programs(1) - 1)
    def _():
        for h in range(H):
            sfin_ref[0, h] = st_ref[h].T


def gla(p, w_gate_up, b_gate, g_gla_out, s0, B, T, tb, C):
    H, DK, DV = GLA_HEADS, GLA_DK, GLA_DV
    nt = T // tb
    row = lambda b, t: b * nt + t
    return pl.pallas_call(
        functools.partial(_gla_kernel, tb=tb, C=C),
        grid=(B, nt),
        in_specs=[pl.BlockSpec((tb, H * DK), lambda b, t: (row(b, t), C_GQ // (H * DK))),
                  pl.BlockSpec((tb, H * DK), lambda b, t: (row(b, t), C_GK // (H * DK))),
                  pl.BlockSpec((tb, H * DV), lambda b, t: (row(b, t), C_GV // (H * DV))),
                  pl.BlockSpec((tb, H * DV), lambda b, t: (row(b, t), C_GG // (H * DV))),
                  pl.BlockSpec((tb, LANES), lambda b, t: (row(b, t), C_SMALL // LANES)),
                  pl.BlockSpec((GATE_RANK, H * DK), lambda b, t: (0, 0)),
                  pl.BlockSpec((1, H * DK), lambda b, t: (0, 0)),
                  pl.BlockSpec((1, DV), lambda b, t: (0, 0)),
                  pl.BlockSpec((1, H, DK, DV), lambda b, t: (b, 0, 0, 0))],
        out_specs=[pl.BlockSpec((tb, H * DV), lambda b, t: (row(b, t), 0)),
                   pl.BlockSpec((1, H, DK, DV), lambda b, t: (b, 0, 0, 0))],
        out_shape=[jax.ShapeDtypeStruct((B * T, H * DV), BF16),
                   jax.ShapeDtypeStruct((B, H, DK, DV), F32)],
        scratch_shapes=[pltpu.VMEM((H, DV, DK), F32),
                        pltpu.VMEM((tb, H * DK), F32),
                        pltpu.VMEM((tb, H * DV), F32)],
        compiler_params=_cparams(("arbitrary", "arbitrary")),
        name="gla",
    )(p, p, p, p, p, w_gate_up.astype(BF16), b_gate.reshape(1, -1), g_gla_out.reshape(1, -1), s0)


def _sortable_key(score):
    bits = lax.bitcast_convert_type(score + 0.0, I32)
    return bits ^ ((bits >> 31) & 0x7FFFFFFF)


def _select_threshold(key_ref, n_chunks, ktarget, pos0, rows, unroll=False):
    ck = key_ref.shape[-1]

    def count(pred):
        def body(c, cnt):
            blk = key_ref[c]
            pos = pos0(c) + lax.broadcasted_iota(I32, (rows, ck), 1)
            m = jnp.where(pred(blk, pos), 1, 0)
            for n in range(ck // LANES):
                cnt = cnt + m[:, n * LANES:(n + 1) * LANES]
            return cnt
        cnt = lax.fori_loop(0, n_chunks, body, jnp.zeros((rows, LANES), I32), unroll=unroll)
        return jnp.sum(cnt, axis=-1, keepdims=True)

    def count_ge(thr):
        return count(lambda blk, pos: blk >= thr)

    thr = jnp.where(count_ge(jnp.zeros((rows, 1), I32)) >= ktarget, 0, INT_MIN).astype(I32)

    def bit_step(it, thr):
        cand = thr | jnp.left_shift(jnp.int32(1), 30 - it)
        return jnp.where(count_ge(cand) >= ktarget, cand, thr)

    thr = lax.fori_loop(0, 31, bit_step, thr)
    n_gt = count(lambda blk, pos: blk > thr)
    n_ge = count_ge(thr)
    need = ktarget - n_gt
    extra = jnp.max(n_ge - n_gt - need)

    def tie_cut():
        def step(_, lohi):
            lo, hi = lohi
            mid = lo + ((hi - lo) >> 1)
            c = count(lambda blk, pos: (blk == thr) & (pos <= mid))
            ok = c >= need
            return jnp.where(ok, lo, mid), jnp.where(ok, mid, hi)
        lo = jnp.full((rows, 1), -1, I32)
        hi = jnp.full((rows, 1), 2 ** 30, I32)
        _, hi = lax.fori_loop(0, 31, step, (lo, hi))
        return hi

    cut = lax.cond(extra > 0, tie_cut, lambda: jnp.full((rows, 1), 2 ** 30, I32))
    return thr, cut


PACK16 = 16
I16_MIN = -32768


def _count16(refs, n_chunks, ck, tq, pred):
    def body(c, cnt):
        r0 = pl.multiple_of(c * ck, ck)
        m = jnp.where(pred(*[r[pl.ds(r0, ck), :] for r in refs], r0), jnp.int16(1), jnp.int16(0))
        parts = [m[r * PACK16:(r + 1) * PACK16] for r in range(ck // PACK16)]
        while len(parts) > 1:
            parts = [parts[p] + parts[p + 1] for p in range(0, len(parts), 2)]
        return cnt + parts[0]

    cnt = lax.fori_loop(0, n_chunks, body, jnp.zeros((PACK16, tq), I16))
    return jnp.sum(cnt.astype(I32), axis=0, keepdims=True)


def _kth_largest16(ref, n_chunks, ck, tq, target):
    def count_ge(thr):
        t16 = thr.astype(I16)
        return _count16([ref], n_chunks, ck, tq, lambda blk, r0: blk >= t16)

    thr = jnp.where(count_ge(jnp.zeros((1, tq), I32)) >= target, 0, I16_MIN).astype(I32)

    def bit_step(it, thr):
        cand = thr | jnp.left_shift(jnp.int32(1), 14 - it)
        return jnp.where(count_ge(cand) >= target, cand, thr)

    return lax.fori_loop(0, 15, bit_step, thr)


def _idx_kernel(iqt_ref, wt_ref, ki_ref, bias_ref, hi_ref, lo_ref, *, tq, ck, k_sel):
    i = pl.program_id(0)
    T = hi_ref.shape[0]
    n_all = T // ck
    n_chunks = ((i + 1) * tq + ck - 1) // ck
    t_pos = i * tq + lax.broadcasted_iota(I32, (1, tq), 1)
    w = wt_ref[...]

    def score_chunk(c, carry):
        r0 = pl.multiple_of(c * ck, ck)
        kc = ki_ref[pl.ds(r0, ck), :]
        acc = jnp.zeros((ck, tq), F32)
        for h in range(IDX_HEADS):
            d = jnp.dot(kc, iqt_ref[h * IDX_DIM:(h + 1) * IDX_DIM, :], preferred_element_type=F32)
            acc = acc + jnp.maximum(d, 0.0) * w[h:h + 1, :]
        s_pos = r0 + lax.broadcasted_iota(I32, (ck, tq), 0)
        key = _sortable_key(jnp.where(s_pos <= t_pos, acc, -jnp.inf))
        hi_ref[pl.ds(r0, ck), :] = (key >> 16).astype(I16)
        lo_ref[pl.ds(r0, ck), :] = ((key & 0xFFFF) + I16_MIN).astype(I16)
        return carry

    lax.fori_loop(0, n_chunks, score_chunk, 0)

    ktarget = jnp.minimum(t_pos + 1, k_sel)
    thr_hi = _kth_largest16(hi_ref, n_chunks, ck, tq, ktarget)
    th16 = thr_hi.astype(I16)
    n_gt_hi = _count16([hi_ref], n_chunks, ck, tq, lambda blk, r0: blk > th16)
    r_need = ktarget - n_gt_hi

    def mask_low(c, carry):
        r0 = pl.multiple_of(c * ck, ck)
        lo_ref[pl.ds(r0, ck), :] = jnp.where(hi_ref[pl.ds(r0, ck), :] == th16, lo_ref[pl.ds(r0, ck), :],
                                             jnp.int16(I16_MIN))
        return carry

    lax.fori_loop(0, n_chunks, mask_low, 0)

    thr_lo = _kth_largest16(lo_ref, n_chunks, ck, tq, r_need)
    tl16 = thr_lo.astype(I16)
    n_gt_lo = _count16([lo_ref], n_chunks, ck, tq, lambda blk, r0: blk > tl16)
    one, zero = jnp.int16(1), jnp.int16(0)

    def is_tie(h, l):
        return jnp.where(h == th16, jnp.where(l == tl16, one, zero), zero)

    n_tie = _count16([hi_ref, lo_ref], n_chunks, ck, tq, lambda h, l, r0: is_tie(h, l) > zero)
    need = r_need - n_gt_lo
    extra = jnp.max(n_tie - need)

    def pos16(r0):
        return (r0 + lax.broadcasted_iota(I32, (ck, tq), 0)).astype(I16)

    def write_bias(sel_fn):
        def body(c, carry):
            r0 = pl.multiple_of(c * ck, ck)
            sel = sel_fn(hi_ref[pl.ds(r0, ck), :], lo_ref[pl.ds(r0, ck), :], r0)
            bias_ref[0, pl.ds(r0, ck), :] = jnp.where(sel > zero, jnp.asarray(0.0, BF16), jnp.asarray(MASKED, BF16))
            return carry
        lax.fori_loop(0, n_chunks, body, 0)

    def no_surplus_ties():
        write_bias(lambda h, l, r0: jnp.where(h > th16, one, jnp.where(h == th16, jnp.where(l >= tl16, one, zero), zero)))

    def surplus_ties():
        def step(_, lohi):
            lo, hi = lohi
            mid = lo + ((hi - lo) >> 1)
            m16 = mid.astype(I16)
            c = _count16([hi_ref, lo_ref], n_chunks, ck, tq,
                         lambda h, l, r0: jnp.where(pos16(r0) <= m16, is_tie(h, l), zero) > zero)
            ok = c >= need
            return jnp.where(ok, lo, mid), jnp.where(ok, mid, hi)

        _, cut = lax.fori_loop(0, 15, step, (jnp.full((1, tq), -1, I32), jnp.full((1, tq), T - 1, I32)))
        c16 = cut.astype(I16)

        def sel_fn(h, l, r0):
            tie_ok = jnp.where(pos16(r0) <= c16, one, zero)
            in_lo = jnp.where(l > tl16, one, jnp.where(l == tl16, tie_ok, zero))
            return jnp.where(h > th16, one, jnp.where(h == th16, in_lo, zero))
        write_bias(sel_fn)

    lax.cond(extra > 0, surplus_ties, no_surplus_ties)

    def write_rest(c, carry):
        r0 = pl.multiple_of(c * ck, ck)
        bias_ref[0, pl.ds(r0, ck), :] = jnp.full((ck, tq), MASKED, bias_ref.dtype)
        return carry

    lax.fori_loop(n_chunks, n_all, write_rest, 0)


def idx_select(ct, wt, ki, T, tq, ck):
    k_sel = min(TOPK_MAX, T // 4)
    return pl.pallas_call(
        functools.partial(_idx_kernel, tq=tq, ck=ck, k_sel=k_sel),
        grid=(T // tq,),
        in_specs=[pl.BlockSpec((IDX_HEADS * IDX_DIM, tq), lambda i: (CT_IQ // (IDX_HEADS * IDX_DIM), i)),
                  pl.BlockSpec((IDX_HEADS, tq), lambda i: (0, i)),
                  pl.BlockSpec((T, IDX_DIM), lambda i: (0, 0))],
        out_specs=pl.BlockSpec((1, T, tq), lambda i: (i, 0, 0)),
        out_shape=jax.ShapeDtypeStruct((T // tq, T, tq), BF16),
        scratch_shapes=[pltpu.VMEM((T, tq), I16), pltpu.VMEM((T, tq), I16)],
        compiler_params=pltpu.CompilerParams(dimension_semantics=("arbitrary",), vmem_limit_bytes=IDX_VMEM_LIMIT),
        name="idx_select",
    )(ct, wt, ki)


DSA_DIM_AUG = DSA_DIM + 16
LOG2E = 1.4426950408889634


def _attn_kernel(ii_ref, jj_ref, qt_ref, k_ref, vt_ref, b_ref, o_ref, m_ref, l_ref, acc_ref, *, tq, tk):
    H, D, DA = DSA_HEADS, DSA_DIM, DSA_DIM_AUG
    s_id = pl.program_id(0)
    i, j = ii_ref[s_id], jj_ref[s_id]

    @pl.when(j == 0)
    def _():
        m_ref[...] = jnp.full(m_ref.shape, M_INIT, F32)
        l_ref[...] = jnp.zeros(l_ref.shape, F32)
        acc_ref[...] = jnp.zeros(acc_ref.shape, F32)

    bias = jnp.concatenate([b_ref[n] for n in range(b_ref.shape[0])], axis=1).astype(F32)

    def qk(h):
        return jnp.dot(k_ref[:, h * D:(h + 1) * D], qt_ref[h * D:(h + 1) * D, :], preferred_element_type=F32)

    s_next = qk(0)
    for h in range(H):
        s = s_next
        if h + 1 < H:
            s_next = qk(h + 1)
        s = s * (D ** -0.5 * LOG2E) + bias
        m_prev = m_ref[h:h + 1, :]
        m_new = jnp.maximum(m_prev, jnp.max(s, axis=0, keepdims=True))
        alpha = jnp.exp2(m_prev - m_new)
        pr = jnp.exp2(s - m_new).astype(BF16)
        pv = jnp.dot(vt_ref[h * DA:(h + 1) * DA, :], pr, preferred_element_type=F32)
        acc_ref[h * D:(h + 1) * D, :] = alpha * acc_ref[h * D:(h + 1) * D, :] + pv[:D]
        l_ref[h:h + 1, :] = alpha * l_ref[h:h + 1, :] + pv[D:D + 1]
        m_ref[h:h + 1, :] = m_new

    @pl.when(j == (i * tq + tq - 1) // tk)
    def _():
        for h in range(H):
            o_ref[h * D:(h + 1) * D, :] = (acc_ref[h * D:(h + 1) * D, :] / l_ref[h:h + 1, :]).astype(o_ref.dtype)


def attention(ct, k, vt_aug, bias, T, tq, tk):
    H, D, DA = DSA_HEADS, DSA_DIM, DSA_DIM_AUG
    tqb = bias.shape[2]
    pairs = [(i, j) for i in range(T // tq) for j in range((i * tq + tq - 1) // tk + 1)]
    ii = jnp.asarray([p[0] for p in pairs], I32)
    jj = jnp.asarray([p[1] for p in pairs], I32)
    grid_spec = pltpu.PrefetchScalarGridSpec(
        num_scalar_prefetch=2,
        grid=(len(pairs),),
        in_specs=[pl.BlockSpec((H * D, tq), lambda s, ii, jj: (CT_AQ // (H * D), ii[s])),
                  pl.BlockSpec((tk, H * D), lambda s, ii, jj: (jj[s], 0)),
                  pl.BlockSpec((H * DA, tk), lambda s, ii, jj: (0, jj[s])),
                  pl.BlockSpec((tq // tqb, tk, tqb), lambda s, ii, jj: (ii[s], jj[s], 0))],
        out_specs=pl.BlockSpec((H * D, tq), lambda s, ii, jj: (0, ii[s])),
        scratch_shapes=[pltpu.VMEM((H, tq), F32), pltpu.VMEM((H, tq), F32), pltpu.VMEM((H * D, tq), F32)],
    )
    return pl.pallas_call(
        functools.partial(_attn_kernel, tq=tq, tk=tk),
        grid_spec=grid_spec,
        out_shape=jax.ShapeDtypeStruct((H * D, T), BF16),
        compiler_params=_cparams(("arbitrary",)),
        name="attention",
    )(ii, jj, ct, k, vt_aug, bias)


def _idx_sample_kernel(pt_ref, q_ref, w_ref, kn_ref, cache_ref, bias_ref, kbuf_ref, key_ref, sem, *,
                       n_pages, ck, k_sel, t_dec):
    b = pl.program_id(0)
    n_chunks = key_ref.shape[0]

    def page_copy(pg):
        return pltpu.make_async_copy(cache_ref.at[pt_ref[b, pg]],
                                     kbuf_ref.at[:, pl.ds(pl.multiple_of(pg * PAGE, PAGE), PAGE)], sem)

    def start(pg, c):
        page_copy(pg).start()
        return c

    lax.fori_loop(0, n_pages, start, 0)
    kbuf_ref[:, n_pages * PAGE:(n_pages + 1) * PAGE] = kn_ref[0]
    kbuf_ref[:, (n_pages + 1) * PAGE:] = jnp.zeros((IDX_DIM, ck - PAGE), F32)

    def wait(pg, c):
        page_copy(pg).wait()
        return c

    lax.fori_loop(0, n_pages, wait, 0)

    q = q_ref[0]
    wcol = w_ref[0]
    t_pos = n_pages * PAGE + lax.broadcasted_iota(I32, (t_dec, 1), 0)

    def score_chunk(c, carry):
        kc = kbuf_ref[:, pl.ds(pl.multiple_of(c * ck, ck), ck)].astype(BF16)
        d = jnp.dot(q, kc, preferred_element_type=F32)
        d = jnp.maximum(d, 0.0) * wcol
        sc = d[0:t_dec]
        for h in range(1, IDX_HEADS):
            sc = sc + d[h * t_dec:(h + 1) * t_dec]
        s_pos = c * ck + lax.broadcasted_iota(I32, (t_dec, ck), 1)
        key_ref[c] = _sortable_key(jnp.where(s_pos <= t_pos, sc, -jnp.inf))
        return carry

    lax.fori_loop(0, n_chunks, score_chunk, 0)

    ktarget = jnp.full((t_dec, 1), k_sel, I32)
    thr, cut = _select_threshold(key_ref, n_chunks, ktarget, lambda c: c * ck, t_dec, unroll=True)

    def write_sel(c, carry):
        blk = key_ref[c]
        pos = c * ck + lax.broadcasted_iota(I32, (t_dec, ck), 1)
        sel = (blk > thr) | ((blk == thr) & (pos <= cut))
        bias_ref[0, c] = jnp.where(sel, 0.0, MASKED)
        return carry

    lax.fori_loop(0, n_chunks, write_sel, 0)


def idx_select_sample(iq, iw, kn, cache_idx_kt, page_table, B, t_dec, ck):
    n_pages = page_table.shape[1]
    past = n_pages * PAGE
    k_sel = min(TOPK_MAX, (past + t_dec) // 4)
    n_chunks = past // ck + 1
    iw = iw.reshape(B, t_dec, IDX_HEADS)
    iw = (iw * (IDX_HEADS ** -0.5 * IDX_DIM ** -0.5)).transpose(0, 2, 1).reshape(B, IDX_HEADS * t_dec, 1)
    kn = jnp.pad(kn.reshape(B, t_dec, IDX_DIM).transpose(0, 2, 1), ((0, 0), (0, 0), (0, PAGE - t_dec)))
    grid_spec = pltpu.PrefetchScalarGridSpec(
        num_scalar_prefetch=1,
        grid=(B,),
        in_specs=[pl.BlockSpec((1, IDX_HEADS * t_dec, IDX_DIM), lambda b, pt: (b, 0, 0)),
                  pl.BlockSpec((1, IDX_HEADS * t_dec, 1), lambda b, pt: (b, 0, 0)),
                  pl.BlockSpec((1, IDX_DIM, PAGE), lambda b, pt: (b, 0, 0)),
                  pl.BlockSpec(memory_space=pl.ANY)],
        out_specs=pl.BlockSpec((1, n_chunks, t_dec, ck), lambda b, pt: (b, 0, 0, 0)),
        scratch_shapes=[pltpu.VMEM((IDX_DIM, past + ck), F32),
                        pltpu.VMEM((n_chunks, t_dec, ck), I32),
                        pltpu.SemaphoreType.DMA(())],
    )
    return pl.pallas_call(
        functools.partial(_idx_sample_kernel, n_pages=n_pages, ck=ck, k_sel=k_sel, t_dec=t_dec),
        grid_spec=grid_spec,
        out_shape=jax.ShapeDtypeStruct((B, n_chunks, t_dec, ck), F32),
        compiler_params=_cparams(("arbitrary",)),
        name="idx_select_sample",
    )(page_table, iq, iw, kn, cache_idx_kt)


KEY_GROUP = 64


def _attn_sample_kernel(pt_ref, q_ref, kn_ref, vn_ref, b_ref, hm_ref, ex_ref, *refs, pg, t_dec):
    H, D = DSA_HEADS, DSA_DIM
    k_refs, v_refs = refs[:pg], refs[pg:2 * pg]
    o_ref, m_ref, l_ref, acc_ref = refs[2 * pg:]
    j = pl.program_id(1)
    nj = pl.num_programs(1)
    n_groups = pg * PAGE // KEY_GROUP

    @pl.when(j == 0)
    def _():
        m_ref[...] = jnp.full(m_ref.shape, M_INIT, F32)
        l_ref[...] = jnp.zeros(l_ref.shape, F32)
        acc_ref[...] = jnp.zeros(acc_ref.shape, F32)

    def update(kc, vc):
        b = b_ref[0, 0]
        by_group = jnp.concatenate([b[:, g * KEY_GROUP:(g + 1) * KEY_GROUP] for g in range(n_groups)], axis=0)
        wide = jnp.dot(by_group.astype(BF16), ex_ref[...], preferred_element_type=F32)
        hm = hm_ref[...]
        bias = jnp.concatenate(
            [jnp.concatenate([wide[g * t_dec:(g + 1) * t_dec]] * H, axis=0) + hm for g in range(n_groups)], axis=1)
        s = lax.dot_general(q_ref[0], kc, (((1,), (1,)), ((), ())), preferred_element_type=F32)
        s = s * (D ** -0.5) + bias
        m_prev = m_ref[...]
        m_new = jnp.maximum(m_prev, jnp.max(s, axis=-1, keepdims=True))
        alpha = jnp.exp(m_prev - m_new)
        pr = jnp.exp(s - m_new)
        l_ref[...] = alpha * l_ref[...] + jnp.sum(pr, axis=-1, keepdims=True)
        acc_ref[...] = alpha * acc_ref[...] + jnp.dot(pr.astype(BF16), vc, preferred_element_type=F32)
        m_ref[...] = m_new

    @pl.when(j < nj - 1)
    def _():
        kc = jnp.concatenate([r[0] for r in k_refs], axis=0).astype(BF16)
        vc = jnp.concatenate([r[0] for r in v_refs], axis=0).astype(BF16)
        update(kc, vc)

    @pl.when(j == nj - 1)
    def _():
        pad = jnp.zeros(((pg * PAGE - t_dec) * H, D), F32)
        kc = jnp.concatenate([kn_ref[0], pad], axis=0).astype(BF16)
        vc = jnp.concatenate([vn_ref[0], pad], axis=0).astype(BF16)
        update(kc, vc)
        out = acc_ref[...] / l_ref[...]
        for h in range(H):
            o_ref[0, :, h * D:(h + 1) * D] = out[h * t_dec:(h + 1) * t_dec].astype(o_ref.dtype)


def attention_sample(q, kn, vn, cache_k, cache_v, page_table, bias, B, t_dec, pg):
    H, D = DSA_HEADS, DSA_DIM
    n_pages = page_table.shape[1]
    nj = n_pages // pg + 1
    gl = KEY_GROUP * H
    q_head = jnp.arange(H * t_dec, dtype=I32)[:, None] // t_dec
    k_head = jnp.arange(gl, dtype=I32)[None, :] % H
    head_mask = jnp.where(q_head == k_head, 0.0, MASKED).astype(F32)
    expand = (jnp.arange(KEY_GROUP, dtype=I32)[:, None] == jnp.arange(gl, dtype=I32)[None, :] // H).astype(BF16)

    def page_map(u):
        return lambda b, j, pt: (pt[b, jnp.minimum(j, n_pages // pg - 1) * pg + u], 0, 0)

    page_specs = [pl.BlockSpec((1, PAGE * H, D), page_map(u)) for u in range(pg)]
    grid_spec = pltpu.PrefetchScalarGridSpec(
        num_scalar_prefetch=1,
        grid=(B, nj),
        in_specs=[pl.BlockSpec((1, H * t_dec, D), lambda b, j, pt: (b, 0, 0)),
                  pl.BlockSpec((1, t_dec * H, D), lambda b, j, pt: (b, 0, 0)),
                  pl.BlockSpec((1, t_dec * H, D), lambda b, j, pt: (b, 0, 0)),
                  pl.BlockSpec((1, 1, t_dec, pg * PAGE), lambda b, j, pt: (b, j, 0, 0)),
                  pl.BlockSpec((H * t_dec, gl), lambda b, j, pt: (0, 0)),
                  pl.BlockSpec((KEY_GROUP, gl), lambda b, j, pt: (0, 0))]
        + page_specs + page_specs,
        out_specs=pl.BlockSpec((1, t_dec, H * D), lambda b, j, pt: (b, 0, 0)),
        scratch_shapes=[pltpu.VMEM((H * t_dec, 1), F32), pltpu.VMEM((H * t_dec, 1), F32),
                        pltpu.VMEM((H * t_dec, D), F32)],
    )
    return pl.pallas_call(
        functools.partial(_attn_sample_kernel, pg=pg, t_dec=t_dec),
        grid_spec=grid_spec,
        out_shape=jax.ShapeDtypeStruct((B, t_dec, H * D), F32),
        compiler_params=_cparams(("parallel", "arbitrary")),
        name="attention_sample",
    )(page_table, q, kn, vn, bias, head_mask, expand, *([cache_k] * pg), *([cache_v] * pg))


def _out_proj_kernel(x_ref, a_ref, b_ref, wa_ref, wb_ref, o_ref):
    o_ref[...] = (x_ref[...] + jnp.dot(a_ref[...], wa_ref[...], preferred_element_type=F32)
                  + jnp.dot(b_ref[...], wb_ref[...], preferred_element_type=F32))


def out_proj(x, a, b, wa, wb, tm, tn):
    M, D = x.shape
    Ka, Kb = a.shape[1], b.shape[1]
    return pl.pallas_call(
        _out_proj_kernel,
        grid=(M // tm, D // tn),
        in_specs=[pl.BlockSpec((tm, tn), lambda i, j: (i, j)),
                  pl.BlockSpec((tm, Ka), lambda i, j: (i, 0)),
                  pl.BlockSpec((tm, Kb), lambda i, j: (i, 0)),
                  pl.BlockSpec((Ka, tn), lambda i, j: (0, j)),
                  pl.BlockSpec((Kb, tn), lambda i, j: (0, j))],
        out_specs=pl.BlockSpec((tm, tn), lambda i, j: (i, j)),
        out_shape=jax.ShapeDtypeStruct((M, D), F32),
        compiler_params=_cparams(("parallel", "arbitrary")),
        name="out_proj",
    )(x, a, b, wa, wb)


def _ffn_kernel(x_ref, g_ref, wu_ref, wd_ref, gf_ref, o_ref, h_ref, acc_ref):
    f = pl.program_id(1)

    @pl.when(f == 0)
    def _():
        x = x_ref[...]
        ms = jnp.mean(x * x, axis=-1, keepdims=True)
        h_ref[...] = (x * lax.rsqrt(ms + EPS) * g_ref[...]).astype(BF16)
        acc_ref[...] = jnp.zeros(acc_ref.shape, F32)

    u = jnp.dot(h_ref[...], wu_ref[...], preferred_element_type=F32)
    u = jnp.square(jnp.maximum(u, 0.0)).astype(BF16)
    acc_ref[...] += jnp.dot(u, wd_ref[...], preferred_element_type=F32)

    @pl.when(f == pl.num_programs(1) - 1)
    def _():
        x2 = x_ref[...] + acc_ref[...]
        ms = jnp.mean(x2 * x2, axis=-1, keepdims=True)
        o_ref[...] = x2 * lax.rsqrt(ms + EPS) * gf_ref[...]


def ffn(x, g_ffn, w_up, w_down, g_final, tm, tf):
    M, D = x.shape
    Fd = w_up.shape[1]
    return pl.pallas_call(
        _ffn_kernel,
        grid=(M // tm, Fd // tf),
        in_specs=[pl.BlockSpec((tm, D), lambda i, f: (i, 0)),
                  pl.BlockSpec((1, D), lambda i, f: (0, 0)),
                  pl.BlockSpec((D, tf), lambda i, f: (0, f)),
                  pl.BlockSpec((tf, D), lambda i, f: (f, 0)),
                  pl.BlockSpec((1, D), lambda i, f: (0, 0))],
        out_specs=pl.BlockSpec((tm, D), lambda i, f: (i, 0)),
        out_shape=jax.ShapeDtypeStruct((M, D), F32),
        scratch_shapes=[pltpu.VMEM((tm, D), BF16), pltpu.VMEM((tm, D), F32)],
        compiler_params=_cparams(("parallel", "arbitrary")),
        name="ffn",
    )(x, g_ffn.reshape(1, D), w_up, w_down, g_final.reshape(1, D))


def _pack_w_in(w_in):
    D = w_in.shape[0]
    gq, gk, gv, gg, ga, aq, ak, av, iq, iw, ik = jnp.split(
        w_in, [512, 1024, 2048, 3072, 3088, 4112, 5136, 6160, 7184, 7200], axis=1)
    pad = jnp.zeros((D, PA_WIDTH - C_SMALL - 96), w_in.dtype)
    w_a = jnp.concatenate([gq, gk, gv, gg, ik, ga, iw, pad], axis=1).astype(BF16)
    w_b = jnp.concatenate([ak, av], axis=1).astype(BF16)
    w_ct = jnp.concatenate([aq, av, iq], axis=1).T.astype(BF16)
    return w_a, w_b, w_ct


def _tile(n, pref):
    return pref if n % pref == 0 else n


def kernel(x_prompt, x_sample, cache_k, cache_v, cache_idx_k, state_gla, page_table,
           g_mix, w_in, w_gate_up, b_gate, g_gla_out, w_o, g_ffn, w_up, w_down, g_final):
    depth = w_in.shape[0]
    Bp, Tp, D = x_prompt.shape
    Bs, Ts, _ = x_sample.shape
    assert Bp == 1 and depth == 1
    H, Dh = DSA_HEADS, DSA_DIM
    n_pool = cache_k.shape[1]

    xp = x_prompt.reshape(Bp * Tp, D)
    xs = x_sample.reshape(Bs * Ts, D)
    l = 0
    w_a, w_b, w_ct = _pack_w_in(w_in[l])
    wo_a = w_o[l, :GLA_HEADS * GLA_DV].astype(BF16)
    wo_b = w_o[l, GLA_HEADS * GLA_DV:].astype(BF16)
    wu = w_up[l].astype(BF16)
    wd = w_down[l].astype(BF16)
    idx_scale = IDX_HEADS ** -0.5 * IDX_DIM ** -0.5

    tm = _tile(Tp, 512)
    pa_p = norm_matmul(xp, g_mix[l], w_a, tm, PA_WIDTH // 2)
    k3_p, v3_p, kb_p = norm_matmul_heads(xp, g_mix[l], w_b, tm)
    ct_p = norm_matmul_t(xp, g_mix[l], w_ct, tm, 1024)
    s0 = jnp.zeros((Bp, GLA_HEADS, GLA_DK, GLA_DV), F32)
    o_gla_p, s_fin_p = gla(pa_p, w_gate_up[l], b_gate[l], g_gla_out[l], s0, Bp, Tp, _tile(Tp, 128), 16)
    ik_p = pa_p[:, C_SMALL + SM_IK:C_SMALL + SM_IK + IDX_DIM]
    wt = (pa_p[:, C_SMALL + SM_IW:C_SMALL + SM_IW + IDX_HEADS] * idx_scale).T
    bias = idx_select(ct_p, wt, ik_p.astype(BF16), Tp, _tile(Tp, 256), _tile(Tp, 512))
    vt = ct_p[CT_AV:CT_AV + H * Dh].reshape(H, Dh, Tp)
    vt_aug = jnp.concatenate([vt, jnp.ones((H, DSA_DIM_AUG - Dh, Tp), BF16)], axis=1).reshape(H * DSA_DIM_AUG, Tp)
    o_dsa_p = attention(ct_p, kb_p, vt_aug, bias, Tp, tm, tm).T
    x1p = out_proj(xp, o_gla_p, o_dsa_p, wo_a, wo_b, tm, 1024)
    y_p = ffn(x1p, g_ffn[l], wu, wd, g_final, tm, 1024)

    Ms = Bs * Ts
    pa_s = norm_matmul(xs, g_mix[l], w_a, Ms, PA_WIDTH // 2)
    k3_s, v3_s, _ = norm_matmul_heads(xs, g_mix[l], w_b, Ms)
    ct_s = norm_matmul_t(xs, g_mix[l], w_ct, Ms, 1024)
    o_gla_s, s_fin_s = gla(pa_s, w_gate_up[l], b_gate[l], g_gla_out[l], state_gla[l], Bs, Ts, Ts, Ts)
    ik_s = pa_s[:, C_SMALL + SM_IK:C_SMALL + SM_IK + IDX_DIM]
    iq_s = ct_s[CT_IQ:CT_IQ + IDX_HEADS * IDX_DIM].reshape(IDX_HEADS, IDX_DIM, Bs, Ts)
    iq_s = iq_s.transpose(2, 0, 3, 1).reshape(Bs, IDX_HEADS * Ts, IDX_DIM)
    pg = 8
    bias_s = idx_select_sample(iq_s, pa_s[:, C_SMALL + SM_IW:C_SMALL + SM_IW + IDX_HEADS], ik_s,
                               jnp.swapaxes(cache_idx_k[l], 1, 2), page_table, Bs, Ts, pg * PAGE)
    q_s = ct_s[CT_AQ:CT_AQ + H * Dh].reshape(H, Dh, Bs, Ts).transpose(2, 0, 3, 1).reshape(Bs, H * Ts, Dh)
    o_dsa_s = attention_sample(q_s, k3_s.reshape(Bs, Ts * H, Dh), v3_s.reshape(Bs, Ts * H, Dh),
                               cache_k[l].reshape(n_pool, PAGE * H, Dh), cache_v[l].reshape(n_pool, PAGE * H, Dh),
                               page_table, bias_s, Bs, Ts, pg)
    x1s = out_proj(xs, o_gla_s, o_dsa_s.reshape(Ms, H * Dh).astype(BF16), wo_a, wo_b, Ms, 1024)
    y_s = ffn(x1s, g_ffn[l], wu, wd, g_final, Ms, 512)

    return (y_p.reshape(Bp, Tp, D), y_s.reshape(Bs, Ts, D),
            k3_p.reshape(1, Bp, Tp, H, Dh), v3_p.reshape(1, Bp, Tp, H, Dh),
            ik_p.reshape(1, Bp, Tp, IDX_DIM), s_fin_p[None],
            k3_s.reshape(1, Bs, Ts, H, Dh), v3_s.reshape(1, Bs, Ts, H, Dh),
            ik_s.reshape(1, Bs, Ts, IDX_DIM), s_fin_s[None])
```

```python
import functools

import jax
import jax.numpy as jnp
from jax import lax
from jax.experimental import pallas as pl
from jax.experimental.pallas import tpu as pltpu

F32 = jnp.float32
BF16 = jnp.bfloat16
I32 = jnp.int32
I16 = jnp.int16

EPS = 1e-6
GATE_TAU = 16.0
GLA_HEADS = 4
GLA_DK = 128
GLA_DV = 256
GATE_RANK = 16
DSA_HEADS = 8
DSA_DIM = 128
IDX_HEADS = 16
IDX_DIM = 64
TOPK_MAX = 256
PAGE = 128

LANES = 128
MASKED = -1e30
M_INIT = -1e29
INT_MIN = -(2 ** 31)
VMEM_LIMIT = 52 * 1024 * 1024
IDX_VMEM_LIMIT = 56 * 1024 * 1024

C_GQ, C_GK, C_GV, C_GG = 0, 512, 1024, 2048
C_SMALL = 3072
SM_IK, SM_GA, SM_IW = 0, 64, 80
PA_WIDTH = 3328
CT_AQ, CT_AV, CT_IQ = 0, 1024, 2048
CT_ROWS = 3072


def _cparams(sem):
    return pltpu.CompilerParams(dimension_semantics=sem, vmem_limit_bytes=VMEM_LIMIT)


def _norm_mm_kernel(x_ref, g_ref, w_ref, o_ref, h_ref):
    @pl.when(pl.program_id(1) == 0)
    def _():
        x = x_ref[...]
        ms = jnp.mean(x * x, axis=-1, keepdims=True)
        h_ref[...] = (x * lax.rsqrt(ms + EPS) * g_ref[...]).astype(BF16)

    o_ref[...] = jnp.dot(h_ref[...], w_ref[...], preferred_element_type=F32)


def norm_matmul(x, g, w, tm, tn):
    M, D = x.shape
    N = w.shape[1]
    return pl.pallas_call(
        _norm_mm_kernel,
        grid=(M // tm, N // tn),
        in_specs=[pl.BlockSpec((tm, D), lambda i, j: (i, 0)),
                  pl.BlockSpec((1, D), lambda i, j: (0, 0)),
                  pl.BlockSpec((D, tn), lambda i, j: (0, j))],
        out_specs=pl.BlockSpec((tm, tn), lambda i, j: (i, j)),
        out_shape=jax.ShapeDtypeStruct((M, N), F32),
        scratch_shapes=[pltpu.VMEM((tm, D), BF16)],
        compiler_params=_cparams(("parallel", "arbitrary")),
        name="norm_matmul",
    )(x, g.reshape(1, D), w)


def _norm_mm_heads_kernel(x_ref, g_ref, w_ref, k3_ref, v3_ref, kb_ref, h_ref):
    H, D = DSA_HEADS, DSA_DIM
    j = pl.program_id(1)

    @pl.when(j == 0)
    def _():
        x = x_ref[...]
        ms = jnp.mean(x * x, axis=-1, keepdims=True)
        h_ref[...] = (x * lax.rsqrt(ms + EPS) * g_ref[...]).astype(BF16)

    r = jnp.dot(h_ref[...], w_ref[...], preferred_element_type=F32)

    @pl.when(j == 0)
    def _():
        kb_ref[...] = r.astype(BF16)
        for h in range(H):
            k3_ref[:, h, :] = r[:, h * D:(h + 1) * D]

    @pl.when(j == 1)
    def _():
        for h in range(H):
            v3_ref[:, h, :] = r[:, h * D:(h + 1) * D]


def norm_matmul_heads(x, g, w, tm):
    M, D = x.shape
    H, Dh = DSA_HEADS, DSA_DIM
    return pl.pallas_call(
        _norm_mm_heads_kernel,
        grid=(M // tm, 2),
        in_specs=[pl.BlockSpec((tm, D), lambda i, j: (i, 0)),
                  pl.BlockSpec((1, D), lambda i, j: (0, 0)),
                  pl.BlockSpec((D, H * Dh), lambda i, j: (0, j))],
        out_specs=[pl.BlockSpec((tm, H, Dh), lambda i, j: (i, 0, 0)),
                   pl.BlockSpec((tm, H, Dh), lambda i, j: (i, 0, 0)),
                   pl.BlockSpec((tm, H * Dh), lambda i, j: (i, 0))],
        out_shape=[jax.ShapeDtypeStruct((M, H, Dh), F32), jax.ShapeDtypeStruct((M, H, Dh), F32),
                   jax.ShapeDtypeStruct((M, H * Dh), BF16)],
        scratch_shapes=[pltpu.VMEM((tm, D), BF16)],
        compiler_params=_cparams(("parallel", "arbitrary")),
        name="norm_matmul_heads",
    )(x, g.reshape(1, D), w)


def _norm_mm_t_kernel(x_ref, g_ref, wt_ref, o_ref, h_ref):
    @pl.when(pl.program_id(1) == 0)
    def _():
        x = x_ref[...]
        ms = jnp.mean(x * x, axis=-1, keepdims=True)
        h_ref[...] = (x * lax.rsqrt(ms + EPS) * g_ref[...]).astype(BF16)

    o_ref[...] = lax.dot_general(wt_ref[...], h_ref[...], (((1,), (1,)), ((), ())),
                                 preferred_element_type=F32).astype(o_ref.dtype)


def norm_matmul_t(x, g, wt, tm, tn):
    M, D = x.shape
    N = wt.shape[0]
    return pl.pallas_call(
        _norm_mm_t_kernel,
        grid=(M // tm, N // tn),
        in_specs=[pl.BlockSpec((tm, D), lambda i, j: (i, 0)),
                  pl.BlockSpec((1, D), lambda i, j: (0, 0)),
                  pl.BlockSpec((tn, D), lambda i, j: (j, 0))],
        out_specs=pl.BlockSpec((tn, tm), lambda i, j: (j, i)),
        out_shape=jax.ShapeDtypeStruct((N, M), BF16),
        scratch_shapes=[pltpu.VMEM((tm, D), BF16)],
        compiler_params=_cparams(("parallel", "arbitrary")),
        name="norm_matmul_t",
    )(x, g.reshape(1, D), wt)


def _gla_kernel(q_ref, k_ref, v_ref, g_ref, sm_ref, wg_ref, bg_ref, go_ref, s0_ref,
                o_ref, sfin_ref, st_ref, b_ref, oacc_ref, *, tb, C):
    H, DK, DV = GLA_HEADS, GLA_DK, GLA_DV
    t = pl.program_id(1)

    @pl.when(t == 0)
    def _():
        for h in range(H):
            st_ref[h] = s0_ref[0, h].T

    a_lr = sm_ref[:, SM_GA:SM_GA + GATE_RANK]
    z = jnp.dot(a_lr.astype(BF16), wg_ref[...], preferred_element_type=F32) + bg_ref[...]
    log_a = (jnp.minimum(z, 0.0) - jnp.log1p(jnp.exp(-jnp.abs(z)))) / GATE_TAU
    row = lax.broadcasted_iota(I32, (tb, H * DK), 0) & (C - 1)
    b = log_a
    sh = 1
    while sh < C:
        b = b + jnp.where(row >= sh, pltpu.roll(b, sh, axis=0), 0.0)
        sh *= 2
    b_ref[...] = b

    jj = lax.broadcasted_iota(I32, (C, 1), 0)

    def chunk(c, carry):
        r0 = pl.multiple_of(c * C, C)
        for h in range(H):
            bq = b_ref[pl.ds(r0, C), h * DK:(h + 1) * DK]
            q = q_ref[pl.ds(r0, C), h * DK:(h + 1) * DK] * (DK ** -0.5)
            k = k_ref[pl.ds(r0, C), h * DK:(h + 1) * DK]
            v = v_ref[pl.ds(r0, C), h * DV:(h + 1) * DV]
            st = st_ref[h]
            qe = q * jnp.exp(bq)
            o = lax.dot_general(qe.astype(BF16), st.astype(BF16), (((1,), (1,)), ((), ())),
                                preferred_element_type=F32)
            rows = []
            for i in range(C):
                causal = jj <= i
                d = jnp.where(causal, bq[i:i + 1, :] - bq, 0.0)
                a = jnp.sum(q[i:i + 1, :] * k * jnp.exp(d), axis=-1, keepdims=True)
                a = jnp.where(causal, a, 0.0)
                rows.append(jnp.sum(a * v, axis=0, keepdims=True))
            o = o + jnp.concatenate(rows, axis=0)
            oacc_ref[pl.ds(r0, C), h * DV:(h + 1) * DV] = o
            b_last = bq[C - 1:C, :]
            ke = k * jnp.exp(b_last - bq)
            kv_t = lax.dot_general(v.astype(BF16), ke.astype(BF16), (((0,), (0,)), ((), ())),
                                   preferred_element_type=F32)
            st_ref[h] = st * jnp.exp(b_last) + kv_t
        return carry

    lax.fori_loop(0, tb // C, chunk, 0)

    gate = g_ref[...]
    gate = gate * (1.0 / (1.0 + jnp.exp(-gate)))
    for h in range(H):
        o = oacc_ref[:, h * DV:(h + 1) * DV]
        ms = jnp.mean(o * o, axis=-1, keepdims=True)
        y = o * lax.rsqrt(ms + EPS) * go_ref[...]
        o_ref[:, h * DV:(h + 1) * DV] = (y * gate[:, h * DV:(h + 1) * DV]).astype(o_ref.dtype)

    @pl.when(t == pl.num_programs(1) - 1)
    def _():
        for h in range(H):
            sfin_ref[0, h] = st_ref[h].T


def gla(p, w_gate_up, b_gate, g_gla_out, s0, B, T, tb, C):
    H, DK, DV = GLA_HEADS, GLA_DK, GLA_DV
    nt = T // tb
    row = lambda b, t: b * nt + t
    return pl.pallas_call(
        functools.partial(_gla_kernel, tb=tb, C=C),
        grid=(B, nt),
        in_specs=[pl.BlockSpec((tb, H * DK), lambda b, t: (row(b, t), C_GQ // (H * DK))),
                  pl.BlockSpec((tb, H * DK), lambda b, t: (row(b, t), C_GK // (H * DK))),
                  pl.BlockSpec((tb, H * DV), lambda b, t: (row(b, t), C_GV // (H * DV))),
                  pl.BlockSpec((tb, H * DV), lambda b, t: (row(b, t), C_GG // (H * DV))),
                  pl.BlockSpec((tb, LANES), lambda b, t: (row(b, t), C_SMALL // LANES)),
                  pl.BlockSpec((GATE_RANK, H * DK), lambda b, t: (0, 0)),
                  pl.BlockSpec((1, H * DK), lambda b, t: (0, 0)),
                  pl.BlockSpec((1, DV), lambda b, t: (0, 0)),
                  pl.BlockSpec((1, H, DK, DV), lambda b, t: (b, 0, 0, 0))],
        out_specs=[pl.BlockSpec((tb, H * DV), lambda b, t: (row(b, t), 0)),
                   pl.BlockSpec((1, H, DK, DV), lambda b, t: (b, 0, 0, 0))],
        out_shape=[jax.ShapeDtypeStruct((B * T, H * DV), BF16),
                   jax.ShapeDtypeStruct((B, H, DK, DV), F32)],
        scratch_shapes=[pltpu.VMEM((H, DV, DK), F32),
                        pltpu.VMEM((tb, H * DK), F32),
                        pltpu.VMEM((tb, H * DV), F32)],
        compiler_params=_cparams(("arbitrary", "arbitrary")),
        name="gla",
    )(p, p, p, p, p, w_gate_up.astype(BF16), b_gate.reshape(1, -1), g_gla_out.reshape(1, -1), s0)


def _sortable_key(score):
    bits = lax.bitcast_convert_type(score + 0.0, I32)
    return bits ^ ((bits >> 31) & 0x7FFFFFFF)


def _select_threshold(key_ref, n_chunks, ktarget, pos0, rows, unroll=False):
    ck = key_ref.shape[-1]

    def count(pred):
        def body(c, cnt):
            blk = key_ref[c]
            pos = pos0(c) + lax.broadcasted_iota(I32, (rows, ck), 1)
            m = jnp.where(pred(blk, pos), 1, 0)
            parts = [m[:, n * LANES:(n + 1) * LANES] for n in range(ck // LANES)]
            while len(parts) > 1:
                parts = [parts[p] + parts[p + 1] for p in range(0, len(parts), 2)]
            return cnt + parts[0]
        cnt = lax.fori_loop(0, n_chunks, body, jnp.zeros((rows, LANES), I32), unroll=unroll)
        return jnp.sum(cnt, axis=-1, keepdims=True)

    def count_ge(thr):
        return count(lambda blk, pos: blk >= thr)

    thr = jnp.where(count_ge(jnp.zeros((rows, 1), I32)) >= ktarget, 0, INT_MIN).astype(I32)

    def bit_step(it, thr):
        cand = thr | jnp.left_shift(jnp.int32(1), 30 - it)
        return jnp.where(count_ge(cand) >= ktarget, cand, thr)

    thr = lax.fori_loop(0, 31, bit_step, thr)
    n_gt = count(lambda blk, pos: blk > thr)
    n_ge = count_ge(thr)
    need = ktarget - n_gt
    extra = jnp.max(n_ge - n_gt - need)

    def tie_cut():
        def step(_, lohi):
            lo, hi = lohi
            mid = lo + ((hi - lo) >> 1)
            c = count(lambda blk, pos: (blk == thr) & (pos <= mid))
            ok = c >= need
            return jnp.where(ok, lo, mid), jnp.where(ok, mid, hi)
        lo = jnp.full((rows, 1), -1, I32)
        hi = jnp.full((rows, 1), 2 ** 30, I32)
        _, hi = lax.fori_loop(0, 31, step, (lo, hi))
        return hi

    cut = lax.cond(extra > 0, tie_cut, lambda: jnp.full((rows, 1), 2 ** 30, I32))
    return thr, cut


PACK16 = 16
I16_MIN = -32768


def _count16(refs, n_chunks, ck, tq, pred):
    def body(c, cnt):
        r0 = pl.multiple_of(c * ck, ck)
        m = jnp.where(pred(*[r[pl.ds(r0, ck), :] for r in refs], r0), jnp.int16(1), jnp.int16(0))
        parts = [m[r * PACK16:(r + 1) * PACK16] for r in range(ck // PACK16)]
        while len(parts) > 1:
            parts = [parts[p] + parts[p + 1] for p in range(0, len(parts), 2)]
        return cnt + parts[0]

    cnt = lax.fori_loop(0, n_chunks, body, jnp.zeros((PACK16, tq), I16))
    return jnp.sum(cnt.astype(I32), axis=0, keepdims=True)


def _kth_largest16(ref, n_chunks, ck, tq, target):
    def count_ge(thr):
        t16 = thr.astype(I16)
        return _count16([ref], n_chunks, ck, tq, lambda blk, r0: blk >= t16)

    thr = jnp.where(count_ge(jnp.zeros((1, tq), I32)) >= target, 0, I16_MIN).astype(I32)

    def bit_step(it, thr):
        cand = thr | jnp.left_shift(jnp.int32(1), 14 - it)
        return jnp.where(count_ge(cand) >= target, cand, thr)

    return lax.fori_loop(0, 15, bit_step, thr)


def _idx_kernel(iqt_ref, wt_ref, ki_ref, bias_ref, hi_ref, lo_ref, *, tq, ck, k_sel):
    i = pl.program_id(0)
    T = hi_ref.shape[0]
    n_all = T // ck
    n_chunks = ((i + 1) * tq + ck - 1) // ck
    t_pos = i * tq + lax.broadcasted_iota(I32, (1, tq), 1)
    w = wt_ref[...]

    def score_chunk(c, carry):
        r0 = pl.multiple_of(c * ck, ck)
        kc = ki_ref[pl.ds(r0, ck), :]
        acc = jnp.zeros((ck, tq), F32)
        for h in range(IDX_HEADS):
            d = jnp.dot(kc, iqt_ref[h * IDX_DIM:(h + 1) * IDX_DIM, :], preferred_element_type=F32)
            acc = acc + jnp.maximum(d, 0.0) * w[h:h + 1, :]
        s_pos = r0 + lax.broadcasted_iota(I32, (ck, tq), 0)
        key = _sortable_key(jnp.where(s_pos <= t_pos, acc, -jnp.inf))
        hi_ref[pl.ds(r0, ck), :] = (key >> 16).astype(I16)
        lo_ref[pl.ds(r0, ck), :] = ((key & 0xFFFF) + I16_MIN).astype(I16)
        return carry

    lax.fori_loop(0, n_chunks, score_chunk, 0)

    ktarget = jnp.minimum(t_pos + 1, k_sel)
    thr_hi = _kth_largest16(hi_ref, n_chunks, ck, tq, ktarget)
    th16 = thr_hi.astype(I16)
    n_gt_hi = _count16([hi_ref], n_chunks, ck, tq, lambda blk, r0: blk > th16)
    r_need = ktarget - n_gt_hi

    def mask_low(c, carry):
        r0 = pl.multiple_of(c * ck, ck)
        lo_ref[pl.ds(r0, ck), :] = jnp.where(hi_ref[pl.ds(r0, ck), :] == th16, lo_ref[pl.ds(r0, ck), :],
                                             jnp.int16(I16_MIN))
        return carry

    lax.fori_loop(0, n_chunks, mask_low, 0)

    thr_lo = _kth_largest16(lo_ref, n_chunks, ck, tq, r_need)
    tl16 = thr_lo.astype(I16)
    n_gt_lo = _count16([lo_ref], n_chunks, ck, tq, lambda blk, r0: blk > tl16)
    one, zero = jnp.int16(1), jnp.int16(0)

    def is_tie(h, l):
        return jnp.where(h == th16, jnp.where(l == tl16, one, zero), zero)

    n_tie = _count16([hi_ref, lo_ref], n_chunks, ck, tq, lambda h, l, r0: is_tie(h, l) > zero)
    need = r_need - n_gt_lo
    extra = jnp.max(n_tie - need)

    def pos16(r0):
        return (r0 + lax.broadcasted_iota(I32, (ck, tq), 0)).astype(I16)

    def write_bias(sel_fn):
        def body(c, carry):
            r0 = pl.multiple_of(c * ck, ck)
            sel = sel_fn(hi_ref[pl.ds(r0, ck), :], lo_ref[pl.ds(r0, ck), :], r0)
            bias_ref[0, pl.ds(r0, ck), :] = jnp.where(sel > zero, jnp.asarray(0.0, BF16), jnp.asarray(MASKED, BF16))
            return carry
        lax.fori_loop(0, n_chunks, body, 0)

    def no_surplus_ties():
        write_bias(lambda h, l, r0: jnp.where(h > th16, one, jnp.where(h == th16, jnp.where(l >= tl16, one, zero), zero)))

    def surplus_ties():
        def step(_, lohi):
            lo, hi = lohi
            mid = lo + ((hi - lo) >> 1)
            m16 = mid.astype(I16)
            c = _count16([hi_ref, lo_ref], n_chunks, ck, tq,
                         lambda h, l, r0: jnp.where(pos16(r0) <= m16, is_tie(h, l), zero) > zero)
            ok = c >= need
            return jnp.where(ok, lo, mid), jnp.where(ok, mid, hi)

        _, cut = lax.fori_loop(0, 15, step, (jnp.full((1, tq), -1, I32), jnp.full((1, tq), T - 1, I32)))
        c16 = cut.astype(I16)

        def sel_fn(h, l, r0):
            tie_ok = jnp.where(pos16(r0) <= c16, one, zero)
            in_lo = jnp.where(l > tl16, one, jnp.where(l == tl16, tie_ok, zero))
            return jnp.where(h > th16, one, jnp.where(h == th16, in_lo, zero))
        write_bias(sel_fn)

    lax.cond(extra > 0, surplus_ties, no_surplus_ties)

    def write_rest(c, carry):
        r0 = pl.multiple_of(c * ck, ck)
        bias_ref[0, pl.ds(r0, ck), :] = jnp.full((ck, tq), MASKED, bias_ref.dtype)
        return carry

    lax.fori_loop(n_chunks, n_all, write_rest, 0)


def idx_select(ct, wt, ki, T, tq, ck):
    k_sel = min(TOPK_MAX, T // 4)
    return pl.pallas_call(
        functools.partial(_idx_kernel, tq=tq, ck=ck, k_sel=k_sel),
        grid=(T // tq,),
        in_specs=[pl.BlockSpec((IDX_HEADS * IDX_DIM, tq), lambda i: (CT_IQ // (IDX_HEADS * IDX_DIM), i)),
                  pl.BlockSpec((IDX_HEADS, tq), lambda i: (0, i)),
                  pl.BlockSpec((T, IDX_DIM), lambda i: (0, 0))],
        out_specs=pl.BlockSpec((1, T, tq), lambda i: (i, 0, 0)),
        out_shape=jax.ShapeDtypeStruct((T // tq, T, tq), BF16),
        scratch_shapes=[pltpu.VMEM((T, tq), I16), pltpu.VMEM((T, tq), I16)],
        compiler_params=pltpu.CompilerParams(dimension_semantics=("arbitrary",), vmem_limit_bytes=IDX_VMEM_LIMIT),
        name="idx_select",
    )(ct, wt, ki)


DSA_DIM_AUG = DSA_DIM + 16
LOG2E = 1.4426950408889634
QK_AHEAD = 2


def _attn_kernel(ii_ref, jj_ref, qt_ref, k_ref, vt_ref, b_ref, o_ref, m_ref, l_ref, acc_ref, *, tq, tk):
    H, D, DA = DSA_HEADS, DSA_DIM, DSA_DIM_AUG
    s_id = pl.program_id(0)
    i, j = ii_ref[s_id], jj_ref[s_id]

    @pl.when(j == 0)
    def _():
        m_ref[...] = jnp.full(m_ref.shape, M_INIT, F32)
        l_ref[...] = jnp.zeros(l_ref.shape, F32)
        acc_ref[...] = jnp.zeros(acc_ref.shape, F32)

    bias = jnp.concatenate([b_ref[n] for n in range(b_ref.shape[0])], axis=1).astype(F32)

    def qk(h):
        return jnp.dot(k_ref[:, h * D:(h + 1) * D], qt_ref[h * D:(h + 1) * D, :], preferred_element_type=F32)

    pending = [qk(h) for h in range(QK_AHEAD)]
    for h in range(H):
        s = pending.pop(0)
        if h + QK_AHEAD < H:
            pending.append(qk(h + QK_AHEAD))
        s = s * (D ** -0.5 * LOG2E) + bias
        m_prev = m_ref[h:h + 1, :]
        m_new = jnp.maximum(m_prev, jnp.max(s, axis=0, keepdims=True))
        alpha = jnp.exp2(m_prev - m_new)
        pr = jnp.exp2(s - m_new).astype(BF16)
        pv = jnp.dot(vt_ref[h * DA:(h + 1) * DA, :], pr, preferred_element_type=F32)
        acc_ref[h * D:(h + 1) * D, :] = alpha * acc_ref[h * D:(h + 1) * D, :] + pv[:D]
        l_ref[h:h + 1, :] = alpha * l_ref[h:h + 1, :] + pv[D:D + 1]
        m_ref[h:h + 1, :] = m_new

    @pl.when(j == (i * tq + tq - 1) // tk)
    def _():
        for h in range(H):
            o_ref[h * D:(h + 1) * D, :] = (acc_ref[h * D:(h + 1) * D, :] / l_ref[h:h + 1, :]).astype(o_ref.dtype)


def attention(ct, k, vt_aug, bias, T, tq, tk):
    H, D, DA = DSA_HEADS, DSA_DIM, DSA_DIM_AUG
    tqb = bias.shape[2]
    pairs = [(i, j) for i in range(T // tq) for j in range((i * tq + tq - 1) // tk + 1)]
    ii = jnp.asarray([p[0] for p in pairs], I32)
    jj = jnp.asarray([p[1] for p in pairs], I32)
    grid_spec = pltpu.PrefetchScalarGridSpec(
        num_scalar_prefetch=2,
        grid=(len(pairs),),
        in_specs=[pl.BlockSpec((H * D, tq), lambda s, ii, jj: (CT_AQ // (H * D), ii[s])),
                  pl.BlockSpec((tk, H * D), lambda s, ii, jj: (jj[s], 0)),
                  pl.BlockSpec((H * DA, tk), lambda s, ii, jj: (0, jj[s])),
                  pl.BlockSpec((tq // tqb, tk, tqb), lambda s, ii, jj: (ii[s], jj[s], 0))],
        out_specs=pl.BlockSpec((H * D, tq), lambda s, ii, jj: (0, ii[s])),
        scratch_shapes=[pltpu.VMEM((H, tq), F32), pltpu.VMEM((H, tq), F32), pltpu.VMEM((H * D, tq), F32)],
    )
    return pl.pallas_call(
        functools.partial(_attn_kernel, tq=tq, tk=tk),
        grid_spec=grid_spec,
        out_shape=jax.ShapeDtypeStruct((H * D, T), BF16),
        compiler_params=_cparams(("arbitrary",)),
        name="attention",
    )(ii, jj, ct, k, vt_aug, bias)


def _idx_sample_kernel(pt_ref, q_ref, w_ref, kn_ref, cache_ref, bias_ref, kbuf_ref, key_ref, sem, *,
                       n_pages, ck, k_sel, t_dec):
    b = pl.program_id(0)
    n_chunks = key_ref.shape[0]

    def page_copy(pg):
        return pltpu.make_async_copy(cache_ref.at[pt_ref[b, pg]],
                                     kbuf_ref.at[:, pl.ds(pl.multiple_of(pg * PAGE, PAGE), PAGE)], sem)

    def start(pg, c):
        page_copy(pg).start()
        return c

    lax.fori_loop(0, n_pages, start, 0)
    kbuf_ref[:, n_pages * PAGE:(n_pages + 1) * PAGE] = kn_ref[0]
    kbuf_ref[:, (n_pages + 1) * PAGE:] = jnp.zeros((IDX_DIM, ck - PAGE), F32)

    def wait(pg, c):
        page_copy(pg).wait()
        return c

    lax.fori_loop(0, n_pages, wait, 0)

    q = q_ref[0]
    wcol = w_ref[0]
    t_pos = n_pages * PAGE + lax.broadcasted_iota(I32, (t_dec, 1), 0)

    def score_chunk(c, carry):
        kc = kbuf_ref[:, pl.ds(pl.multiple_of(c * ck, ck), ck)].astype(BF16)
        d = jnp.dot(q, kc, preferred_element_type=F32)
        d = jnp.maximum(d, 0.0) * wcol
        sc = d[0:t_dec]
        for h in range(1, IDX_HEADS):
            sc = sc + d[h * t_dec:(h + 1) * t_dec]
        s_pos = c * ck + lax.broadcasted_iota(I32, (t_dec, ck), 1)
        key_ref[c] = _sortable_key(jnp.where(s_pos <= t_pos, sc, -jnp.inf))
        return carry

    lax.fori_loop(0, n_chunks, score_chunk, 0)

    ktarget = jnp.full((t_dec, 1), k_sel, I32)
    thr, cut = _select_threshold(key_ref, n_chunks, ktarget, lambda c: c * ck, t_dec, unroll=True)

    def write_sel(c, carry):
        blk = key_ref[c]
        pos = c * ck + lax.broadcasted_iota(I32, (t_dec, ck), 1)
        sel = (blk > thr) | ((blk == thr) & (pos <= cut))
        bias_ref[0, c] = jnp.where(sel, 0.0, MASKED)
        return carry

    lax.fori_loop(0, n_chunks, write_sel, 0)


def idx_select_sample(iq, iw, kn, cache_idx_kt, page_table, B, t_dec, ck):
    n_pages = page_table.shape[1]
    past = n_pages * PAGE
    k_sel = min(TOPK_MAX, (past + t_dec) // 4)
    n_chunks = past // ck + 1
    iw = iw.reshape(B, t_dec, IDX_HEADS)
    iw = (iw * (IDX_HEADS ** -0.5 * IDX_DIM ** -0.5)).transpose(0, 2, 1).reshape(B, IDX_HEADS * t_dec, 1)
    kn = jnp.pad(kn.reshape(B, t_dec, IDX_DIM).transpose(0, 2, 1), ((0, 0), (0, 0), (0, PAGE - t_dec)))
    grid_spec = pltpu.PrefetchScalarGridSpec(
        num_scalar_prefetch=1,
        grid=(B,),
        in_specs=[pl.BlockSpec((1, IDX_HEADS * t_dec, IDX_DIM), lambda b, pt: (b, 0, 0)),
                  pl.BlockSpec((1, IDX_HEADS * t_dec, 1), lambda b, pt: (b, 0, 0)),
                  pl.BlockSpec((1, IDX_DIM, PAGE), lambda b, pt: (b, 0, 0)),
                  pl.BlockSpec(memory_space=pl.ANY)],
        out_specs=pl.BlockSpec((1, n_chunks, t_dec, ck), lambda b, pt: (b, 0, 0, 0)),
        scratch_shapes=[pltpu.VMEM((IDX_DIM, past + ck), F32),
                        pltpu.VMEM((n_chunks, t_dec, ck), I32),
                        pltpu.SemaphoreType.DMA(())],
    )
    return pl.pallas_call(
        functools.partial(_idx_sample_kernel, n_pages=n_pages, ck=ck, k_sel=k_sel, t_dec=t_dec),
        grid_spec=grid_spec,
        out_shape=jax.ShapeDtypeStruct((B, n_chunks, t_dec, ck), F32),
        compiler_params=_cparams(("arbitrary",)),
        name="idx_select_sample",
    )(page_table, iq, iw, kn, cache_idx_kt)


KEY_GROUP = 64


def _attn_sample_kernel(pt_ref, q_ref, kn_ref, vn_ref, b_ref, hm_ref, ex_ref, *refs, pg, t_dec):
    H, D = DSA_HEADS, DSA_DIM
    k_refs, v_refs = refs[:pg], refs[pg:2 * pg]
    o_ref, m_ref, l_ref, acc_ref = refs[2 * pg:]
    j = pl.program_id(1)
    nj = pl.num_programs(1)
    n_groups = pg * PAGE // KEY_GROUP

    @pl.when(j == 0)
    def _():
        m_ref[...] = jnp.full(m_ref.shape, M_INIT, F32)
        l_ref[...] = jnp.zeros(l_ref.shape, F32)
        acc_ref[...] = jnp.zeros(acc_ref.shape, F32)

    def update(kc, vc):
        b = b_ref[0, 0]
        by_group = jnp.concatenate([b[:, g * KEY_GROUP:(g + 1) * KEY_GROUP] for g in range(n_groups)], axis=0)
        wide = jnp.dot(by_group.astype(BF16), ex_ref[...], preferred_element_type=F32)
        hm = hm_ref[...]
        bias = jnp.concatenate(
            [jnp.concatenate([wide[g * t_dec:(g + 1) * t_dec]] * H, axis=0) + hm for g in range(n_groups)], axis=1)
        s = lax.dot_general(q_ref[0], kc, (((1,), (1,)), ((), ())), preferred_element_type=F32)
        s = s * (D ** -0.5) + bias
        m_prev = m_ref[...]
        m_new = jnp.maximum(m_prev, jnp.max(s, axis=-1, keepdims=True))
        alpha = jnp.exp(m_prev - m_new)
        pr = jnp.exp(s - m_new)
        l_ref[...] = alpha * l_ref[...] + jnp.sum(pr, axis=-1, keepdims=True)
        acc_ref[...] = alpha * acc_ref[...] + jnp.dot(pr.astype(BF16), vc, preferred_element_type=F32)
        m_ref[...] = m_new

    @pl.when(j < nj - 1)
    def _():
        kc = jnp.concatenate([r[0] for r in k_refs], axis=0).astype(BF16)
        vc = jnp.concatenate([r[0] for r in v_refs], axis=0).astype(BF16)
        update(kc, vc)

    @pl.when(j == nj - 1)
    def _():
        pad = jnp.zeros(((pg * PAGE - t_dec) * H, D), F32)
        kc = jnp.concatenate([kn_ref[0], pad], axis=0).astype(BF16)
        vc = jnp.concatenate([vn_ref[0], pad], axis=0).astype(BF16)
        update(kc, vc)
        out = acc_ref[...] / l_ref[...]
        for h in range(H):
            o_ref[0, :, h * D:(h + 1) * D] = out[h * t_dec:(h + 1) * t_dec].astype(o_ref.dtype)


def attention_sample(q, kn, vn, cache_k, cache_v, page_table, bias, B, t_dec, pg):
    H, D = DSA_HEADS, DSA_DIM
    n_pages = page_table.shape[1]
    nj = n_pages // pg + 1
    gl = KEY_GROUP * H
    q_head = jnp.arange(H * t_dec, dtype=I32)[:, None] // t_dec
    k_head = jnp.arange(gl, dtype=I32)[None, :] % H
    head_mask = jnp.where(q_head == k_head, 0.0, MASKED).astype(F32)
    expand = (jnp.arange(KEY_GROUP, dtype=I32)[:, None] == jnp.arange(gl, dtype=I32)[None, :] // H).astype(BF16)

    def page_map(u):
        return lambda b, j, pt: (pt[b, jnp.minimum(j, n_pages // pg - 1) * pg + u], 0, 0)

    page_specs = [pl.BlockSpec((1, PAGE * H, D), page_map(u)) for u in range(pg)]
    grid_spec = pltpu.PrefetchScalarGridSpec(
        num_scalar_prefetch=1,
        grid=(B, nj),
        in_specs=[pl.BlockSpec((1, H * t_dec, D), lambda b, j, pt: (b, 0, 0)),
                  pl.BlockSpec((1, t_dec * H, D), lambda b, j, pt: (b, 0, 0)),
                  pl.BlockSpec((1, t_dec * H, D), lambda b, j, pt: (b, 0, 0)),
                  pl.BlockSpec((1, 1, t_dec, pg * PAGE), lambda b, j, pt: (b, j, 0, 0)),
                  pl.BlockSpec((H * t_dec, gl), lambda b, j, pt: (0, 0)),
                  pl.BlockSpec((KEY_GROUP, gl), lambda b, j, pt: (0, 0))]
        + page_specs + page_specs,
        out_specs=pl.BlockSpec((1, t_dec, H * D), lambda b, j, pt: (b, 0, 0)),
        scratch_shapes=[pltpu.VMEM((H * t_dec, 1), F32), pltpu.VMEM((H * t_dec, 1), F32),
                        pltpu.VMEM((H * t_dec, D), F32)],
    )
    return pl.pallas_call(
        functools.partial(_attn_sample_kernel, pg=pg, t_dec=t_dec),
        grid_spec=grid_spec,
        out_shape=jax.ShapeDtypeStruct((B, t_dec, H * D), F32),
        compiler_params=_cparams(("parallel", "arbitrary")),
        name="attention_sample",
    )(page_table, q, kn, vn, bias, head_mask, expand, *([cache_k] * pg), *([cache_v] * pg))


def _out_proj_kernel(x_ref, a_ref, b_ref, wa_ref, wb_ref, o_ref):
    o_ref[...] = (x_ref[...] + jnp.dot(a_ref[...], wa_ref[...], preferred_element_type=F32)
                  + jnp.dot(b_ref[...], wb_ref[...], preferred_element_type=F32))


def out_proj(x, a, b, wa, wb, tm, tn):
    M, D = x.shape
    Ka, Kb = a.shape[1], b.shape[1]
    return pl.pallas_call(
        _out_proj_kernel,
        grid=(M // tm, D // tn),
        in_specs=[pl.BlockSpec((tm, tn), lambda i, j: (i, j)),
                  pl.BlockSpec((tm, Ka), lambda i, j: (i, 0)),
                  pl.BlockSpec((tm, Kb), lambda i, j: (i, 0)),
                  pl.BlockSpec((Ka, tn), lambda i, j: (0, j)),
                  pl.BlockSpec((Kb, tn), lambda i, j: (0, j))],
        out_specs=pl.BlockSpec((tm, tn), lambda i, j: (i, j)),
        out_shape=jax.ShapeDtypeStruct((M, D), F32),
        compiler_params=_cparams(("parallel", "arbitrary")),
        name="out_proj",
    )(x, a, b, wa, wb)


def _ffn_kernel(x_ref, g_ref, wu_ref, wd_ref, gf_ref, o_ref, h_ref, acc_ref):
    f = pl.program_id(1)

    @pl.when(f == 0)
    def _():
        x = x_ref[...]
        ms = jnp.mean(x * x, axis=-1, keepdims=True)
        h_ref[...] = (x * lax.rsqrt(ms + EPS) * g_ref[...]).astype(BF16)
        acc_ref[...] = jnp.zeros(acc_ref.shape, F32)

    u = jnp.dot(h_ref[...], wu_ref[...], preferred_element_type=F32)
    u = jnp.square(jnp.maximum(u, 0.0)).astype(BF16)
    acc_ref[...] += jnp.dot(u, wd_ref[...], preferred_element_type=F32)

    @pl.when(f == pl.num_programs(1) - 1)
    def _():
        x2 = x_ref[...] + acc_ref[...]
        ms = jnp.mean(x2 * x2, axis=-1, keepdims=True)
        o_ref[...] = x2 * lax.rsqrt(ms + EPS) * gf_ref[...]


def ffn(x, g_ffn, w_up, w_down, g_final, tm, tf):
    M, D = x.shape
    Fd = w_up.shape[1]
    return pl.pallas_call(
        _ffn_kernel,
        grid=(M // tm, Fd // tf),
        in_specs=[pl.BlockSpec((tm, D), lambda i, f: (i, 0)),
                  pl.BlockSpec((1, D), lambda i, f: (0, 0)),
                  pl.BlockSpec((D, tf), lambda i, f: (0, f)),
                  pl.BlockSpec((tf, D), lambda i, f: (f, 0)),
                  pl.BlockSpec((1, D), lambda i, f: (0, 0))],
        out_specs=pl.BlockSpec((tm, D), lambda i, f: (i, 0)),
        out_shape=jax.ShapeDtypeStruct((M, D), F32),
        scratch_shapes=[pltpu.VMEM((tm, D), BF16), pltpu.VMEM((tm, D), F32)],
        compiler_params=_cparams(("parallel", "arbitrary")),
        name="ffn",
    )(x, g_ffn.reshape(1, D), w_up, w_down, g_final.reshape(1, D))


def _pack_w_in(w_in):
    D = w_in.shape[0]
    gq, gk, gv, gg, ga, aq, ak, av, iq, iw, ik = jnp.split(
        w_in, [512, 1024, 2048, 3072, 3088, 4112, 5136, 6160, 7184, 7200], axis=1)
    pad = jnp.zeros((D, PA_WIDTH - C_SMALL - 96), w_in.dtype)
    w_a = jnp.concatenate([gq, gk, gv, gg, ik, ga, iw, pad], axis=1).astype(BF16)
    w_b = jnp.concatenate([ak, av], axis=1).astype(BF16)
    w_ct = jnp.concatenate([aq, av, iq], axis=1).T.astype(BF16)
    return w_a, w_b, w_ct


def _tile(n, pref):
    return pref if n % pref == 0 else n


def kernel(x_prompt, x_sample, cache_k, cache_v, cache_idx_k, state_gla, page_table,
           g_mix, w_in, w_gate_up, b_gate, g_gla_out, w_o, g_ffn, w_up, w_down, g_final):
    depth = w_in.shape[0]
    Bp, Tp, D = x_prompt.shape
    Bs, Ts, _ = x_sample.shape
    assert Bp == 1 and depth == 1
    H, Dh = DSA_HEADS, DSA_DIM
    n_pool = cache_k.shape[1]

    xp = x_prompt.reshape(Bp * Tp, D)
    xs = x_sample.reshape(Bs * Ts, D)
    l = 0
    w_a, w_b, w_ct = _pack_w_in(w_in[l])
    wo_a = w_o[l, :GLA_HEADS * GLA_DV].astype(BF16)
    wo_b = w_o[l, GLA_HEADS * GLA_DV:].astype(BF16)
    wu = w_up[l].astype(BF16)
    wd = w_down[l].astype(BF16)
    idx_scale = IDX_HEADS ** -0.5 * IDX_DIM ** -0.5

    tm = _tile(Tp, 512)
    pa_p = norm_matmul(xp, g_mix[l], w_a, tm, PA_WIDTH // 2)
    k3_p, v3_p, kb_p = norm_matmul_heads(xp, g_mix[l], w_b, tm)
    ct_p = norm_matmul_t(xp, g_mix[l], w_ct, tm, 1024)
    s0 = jnp.zeros((Bp, GLA_HEADS, GLA_DK, GLA_DV), F32)
    o_gla_p, s_fin_p = gla(pa_p, w_gate_up[l], b_gate[l], g_gla_out[l], s0, Bp, Tp, _tile(Tp, 128), 16)
    ik_p = pa_p[:, C_SMALL + SM_IK:C_SMALL + SM_IK + IDX_DIM]
    wt = (pa_p[:, C_SMALL + SM_IW:C_SMALL + SM_IW + IDX_HEADS] * idx_scale).T
    bias = idx_select(ct_p, wt, ik_p.astype(BF16), Tp, _tile(Tp, 256), _tile(Tp, 512))
    vt = ct_p[CT_AV:CT_AV + H * Dh].reshape(H, Dh, Tp)
    vt_aug = jnp.concatenate([vt, jnp.ones((H, DSA_DIM_AUG - Dh, Tp), BF16)], axis=1).reshape(H * DSA_DIM_AUG, Tp)
    o_dsa_p = attention(ct_p, kb_p, vt_aug, bias, Tp, tm, _tile(Tp, 1024)).T
    x1p = out_proj(xp, o_gla_p, o_dsa_p, wo_a, wo_b, tm, 1024)
    y_p = ffn(x1p, g_ffn[l], wu, wd, g_final, tm, 1024)

    Ms = Bs * Ts
    pa_s = norm_matmul(xs, g_mix[l], w_a, Ms, PA_WIDTH // 2)
    k3_s, v3_s, _ = norm_matmul_heads(xs, g_mix[l], w_b, Ms)
    ct_s = norm_matmul_t(xs, g_mix[l], w_ct, Ms, 1024)
    o_gla_s, s_fin_s = gla(pa_s, w_gate_up[l], b_gate[l], g_gla_out[l], state_gla[l], Bs, Ts, Ts, Ts)
    ik_s = pa_s[:, C_SMALL + SM_IK:C_SMALL + SM_IK + IDX_DIM]
    iq_s = ct_s[CT_IQ:CT_IQ + IDX_HEADS * IDX_DIM].reshape(IDX_HEADS, IDX_DIM, Bs, Ts)
    iq_s = iq_s.transpose(2, 0, 3, 1).reshape(Bs, IDX_HEADS * Ts, IDX_DIM)
    pg = 8
    bias_s = idx_select_sample(iq_s, pa_s[:, C_SMALL + SM_IW:C_SMALL + SM_IW + IDX_HEADS], ik_s,
                               jnp.swapaxes(cache_idx_k[l], 1, 2), page_table, Bs, Ts, pg * PAGE)
    q_s = ct_s[CT_AQ:CT_AQ + H * Dh].reshape(H, Dh, Bs, Ts).transpose(2, 0, 3, 1).reshape(Bs, H * Ts, Dh)
    o_dsa_s = attention_sample(q_s, k3_s.reshape(Bs, Ts * H, Dh), v3_s.reshape(Bs, Ts * H, Dh),
                               cache_k[l].reshape(n_pool, PAGE * H, Dh), cache_v[l].reshape(n_pool, PAGE * H, Dh),
                               page_table, bias_s, Bs, Ts, pg)
    x1s = out_proj(xs, o_gla_s, o_dsa_s.reshape(Ms, H * Dh).astype(BF16), wo_a, wo_b, Ms, 1024)
    y_s = ffn(x1s, g_ffn[l], wu, wd, g_final, Ms, 512)

    return (y_p.reshape(Bp, Tp, D), y_s.reshape(Bs, Ts, D),
            k3_p.reshape(1, Bp, Tp, H, Dh), v3_p.reshape(1, Bp, Tp, H, Dh),
            ik_p.reshape(1, Bp, Tp, IDX_DIM), s_fin_p[None],
            k3_s.reshape(1, Bs, Ts, H, Dh), v3_s.reshape(1, Bs, Ts, H, Dh),
            ik_s.reshape(1, Bs, Ts, IDX_DIM), s_fin_s[None])
```

```python
import functools
import math

import jax
import jax.numpy as jnp
from jax import lax
from jax.experimental import pallas as pl
from jax.experimental.pallas import tpu as pltpu

F32 = jnp.float32
BF16 = jnp.bfloat16
I32 = jnp.int32

EPS = 1e-6
GATE_TAU = 16.0
GLA_HEADS = 4
GLA_DK = 128
GLA_DV = 256
GATE_RANK = 16
DSA_HEADS = 8
DSA_DIM = 128
IDX_HEADS = 16
IDX_DIM = 64
TOPK_MAX = 256
PAGE = 128

LANES = 128
MASKED = -1e30
M_INIT = -1e29
INT_MIN = -(2 ** 31)
VMEM_LIMIT = 52 * 1024 * 1024
IDX_VMEM_LIMIT = 56 * 1024 * 1024

C_GQ, C_GK, C_GV, C_GG = 0, 512, 1024, 2048
C_SMALL = 3072
SM_IK, SM_GA, SM_IW = 0, 64, 80
PA_WIDTH = 3328
CT_AQ, CT_AV, CT_IQ = 0, 1024, 2048
CT_ROWS = 3072


def _cparams(sem):
    return pltpu.CompilerParams(dimension_semantics=sem, vmem_limit_bytes=VMEM_LIMIT)


def _norm_mm_kernel(x_ref, g_ref, w_ref, o_ref, h_ref):
    @pl.when(pl.program_id(1) == 0)
    def _():
        x = x_ref[...]
        ms = jnp.mean(x * x, axis=-1, keepdims=True)
        h_ref[...] = (x * lax.rsqrt(ms + EPS) * g_ref[...]).astype(BF16)

    o_ref[...] = jnp.dot(h_ref[...], w_ref[...], preferred_element_type=F32)


def norm_matmul(x, g, w, tm, tn):
    M, D = x.shape
    N = w.shape[1]
    return pl.pallas_call(
        _norm_mm_kernel,
        grid=(M // tm, N // tn),
        in_specs=[pl.BlockSpec((tm, D), lambda i, j: (i, 0)),
                  pl.BlockSpec((1, D), lambda i, j: (0, 0)),
                  pl.BlockSpec((D, tn), lambda i, j: (0, j))],
        out_specs=pl.BlockSpec((tm, tn), lambda i, j: (i, j)),
        out_shape=jax.ShapeDtypeStruct((M, N), F32),
        scratch_shapes=[pltpu.VMEM((tm, D), BF16)],
        compiler_params=_cparams(("parallel", "arbitrary")),
        name="norm_matmul",
    )(x, g.reshape(1, D), w)


def _norm_mm_heads_kernel(x_ref, g_ref, w_ref, k3_ref, v3_ref, kb_ref, h_ref):
    H, D = DSA_HEADS, DSA_DIM
    j = pl.program_id(1)

    @pl.when(j == 0)
    def _():
        x = x_ref[...]
        ms = jnp.mean(x * x, axis=-1, keepdims=True)
        h_ref[...] = (x * lax.rsqrt(ms + EPS) * g_ref[...]).astype(BF16)

    r = jnp.dot(h_ref[...], w_ref[...], preferred_element_type=F32)

    @pl.when(j == 0)
    def _():
        kb_ref[...] = r.astype(BF16)
        for h in range(H):
            k3_ref[:, h, :] = r[:, h * D:(h + 1) * D]

    @pl.when(j == 1)
    def _():
        for h in range(H):
            v3_ref[:, h, :] = r[:, h * D:(h + 1) * D]


def norm_matmul_heads(x, g, w, tm):
    M, D = x.shape
    H, Dh = DSA_HEADS, DSA_DIM
    return pl.pallas_call(
        _norm_mm_heads_kernel,
        grid=(M // tm, 2),
        in_specs=[pl.BlockSpec((tm, D), lambda i, j: (i, 0)),
                  pl.BlockSpec((1, D), lambda i, j: (0, 0)),
                  pl.BlockSpec((D, H * Dh), lambda i, j: (0, j))],
        out_specs=[pl.BlockSpec((tm, H, Dh), lambda i, j: (i, 0, 0)),
                   pl.BlockSpec((tm, H, Dh), lambda i, j: (i, 0, 0)),
                   pl.BlockSpec((tm, H * Dh), lambda i, j: (i, 0))],
        out_shape=[jax.ShapeDtypeStruct((M, H, Dh), F32), jax.ShapeDtypeStruct((M, H, Dh), F32),
                   jax.ShapeDtypeStruct((M, H * Dh), BF16)],
        scratch_shapes=[pltpu.VMEM((tm, D), BF16)],
        compiler_params=_cparams(("parallel", "arbitrary")),
        name="norm_matmul_heads",
    )(x, g.reshape(1, D), w)


def _norm_mm_t_kernel(x_ref, g_ref, wt_ref, o_ref, h_ref):
    @pl.when(pl.program_id(1) == 0)
    def _():
        x = x_ref[...]
        ms = jnp.mean(x * x, axis=-1, keepdims=True)
        h_ref[...] = (x * lax.rsqrt(ms + EPS) * g_ref[...]).astype(BF16)

    o_ref[...] = lax.dot_general(wt_ref[...], h_ref[...], (((1,), (1,)), ((), ())),
                                 preferred_element_type=F32).astype(o_ref.dtype)


def norm_matmul_t(x, g, wt, tm, tn):
    M, D = x.shape
    N = wt.shape[0]
    return pl.pallas_call(
        _norm_mm_t_kernel,
        grid=(M // tm, N // tn),
        in_specs=[pl.BlockSpec((tm, D), lambda i, j: (i, 0)),
                  pl.BlockSpec((1, D), lambda i, j: (0, 0)),
                  pl.BlockSpec((tn, D), lambda i, j: (j, 0))],
        out_specs=pl.BlockSpec((tn, tm), lambda i, j: (j, i)),
        out_shape=jax.ShapeDtypeStruct((N, M), BF16),
        scratch_shapes=[pltpu.VMEM((tm, D), BF16)],
        compiler_params=_cparams(("parallel", "arbitrary")),
        name="norm_matmul_t",
    )(x, g.reshape(1, D), wt)


def _gla_kernel(q_ref, k_ref, v_ref, g_ref, sm_ref, wg_ref, bg_ref, go_ref, s0_ref,
                o_ref, sfin_ref, st_ref, b_ref, oacc_ref, *, tb, C):
    H, DK, DV = GLA_HEADS, GLA_DK, GLA_DV
    t = pl.program_id(1)

    @pl.when(t == 0)
    def _():
        for h in range(H):
            st_ref[h] = s0_ref[0, h].T

    a_lr = sm_ref[:, SM_GA:SM_GA + GATE_RANK]
    z = jnp.dot(a_lr.astype(BF16), wg_ref[...], preferred_element_type=F32) + bg_ref[...]
    log_a = (jnp.minimum(z, 0.0) - jnp.log1p(jnp.exp(-jnp.abs(z)))) / GATE_TAU
    row = lax.broadcasted_iota(I32, (tb, H * DK), 0) & (C - 1)
    b = log_a
    sh = 1
    while sh < C:
        b = b + jnp.where(row >= sh, pltpu.roll(b, sh, axis=0), 0.0)
        sh *= 2
    b_ref[...] = b

    jj = lax.broadcasted_iota(I32, (C, 1), 0)

    def chunk(c, carry):
        r0 = pl.multiple_of(c * C, C)
        for h in range(H):
            bq = b_ref[pl.ds(r0, C), h * DK:(h + 1) * DK]
            q = q_ref[pl.ds(r0, C), h * DK:(h + 1) * DK] * (DK ** -0.5)
            k = k_ref[pl.ds(r0, C), h * DK:(h + 1) * DK]
            v = v_ref[pl.ds(r0, C), h * DV:(h + 1) * DV]
            st = st_ref[h]
            qe = q * jnp.exp(bq)
            o = lax.dot_general(qe.astype(BF16), st.astype(BF16), (((1,), (1,)), ((), ())),
                                preferred_element_type=F32)
            rows = []
            for i in range(C):
                causal = jj <= i
                d = jnp.where(causal, bq[i:i + 1, :] - bq, 0.0)
                a = jnp.sum(q[i:i + 1, :] * k * jnp.exp(d), axis=-1, keepdims=True)
                a = jnp.where(causal, a, 0.0)
                rows.append(jnp.sum(a * v, axis=0, keepdims=True))
            o = o + jnp.concatenate(rows, axis=0)
            oacc_ref[pl.ds(r0, C), h * DV:(h + 1) * DV] = o
            b_last = bq[C - 1:C, :]
            ke = k * jnp.exp(b_last - bq)
            kv_t = lax.dot_general(v.astype(BF16), ke.astype(BF16), (((0,), (0,)), ((), ())),
                                   preferred_element_type=F32)
            st_ref[h] = st * jnp.exp(b_last) + kv_t
        return carry

    lax.fori_loop(0, tb // C, chunk, 0)

    gate = g_ref[...]
    gate = gate * (1.0 / (1.0 + jnp.exp(-gate)))
    for h in range(H):
        o = oacc_ref[:, h * DV:(h + 1) * DV]
        ms = jnp.mean(o * o, axis=-1, keepdims=True)
        y = o * lax.rsqrt(ms + EPS) * go_ref[...]
        o_ref[:, h * DV:(h + 1) * DV] = (y * gate[:, h * DV:(h + 1) * DV]).astype(o_ref.dtype)

    @pl.when(t == pl.num_programs(1) - 1)
    def _():
        for h in range(H):
            sfin_ref[0, h] = st_ref[h].T


def gla(p, w_gate_up, b_gate, g_gla_out, s0, B, T, tb, C):
    H, DK, DV = GLA_HEADS, GLA_DK, GLA_DV
    nt = T // tb
    row = lambda b, t: b * nt + t
    return pl.pallas_call(
        functools.partial(_gla_kernel, tb=tb, C=C),
        grid=(B, nt),
        in_specs=[pl.BlockSpec((tb, H * DK), lambda b, t: (row(b, t), C_GQ // (H * DK))),
                  pl.BlockSpec((tb, H * DK), lambda b, t: (row(b, t), C_GK // (H * DK))),
                  pl.BlockSpec((tb, H * DV), lambda b, t: (row(b, t), C_GV // (H * DV))),
                  pl.BlockSpec((tb, H * DV), lambda b, t: (row(b, t), C_GG // (H * DV))),
                  pl.BlockSpec((tb, LANES), lambda b, t: (row(b, t), C_SMALL // LANES)),
                  pl.BlockSpec((GATE_RANK, H * DK), lambda b, t: (0, 0)),
                  pl.BlockSpec((1, H * DK), lambda b, t: (0, 0)),
                  pl.BlockSpec((1, DV), lambda b, t: (0, 0)),
                  pl.BlockSpec((1, H, DK, DV), lambda b, t: (b, 0, 0, 0))],
        out_specs=[pl.BlockSpec((tb, H * DV), lambda b, t: (row(b, t), 0)),
                   pl.BlockSpec((1, H, DK, DV), lambda b, t: (b, 0, 0, 0))],
        out_shape=[jax.ShapeDtypeStruct((B * T, H * DV), BF16),
                   jax.ShapeDtypeStruct((B, H, DK, DV), F32)],
        scratch_shapes=[pltpu.VMEM((H, DV, DK), F32),
                        pltpu.VMEM((tb, H * DK), F32),
                        pltpu.VMEM((tb, H * DV), F32)],
        compiler_params=_cparams(("arbitrary", "arbitrary")),
        name="gla",
    )(p, p, p, p, p, w_gate_up.astype(BF16), b_gate.reshape(1, -1), g_gla_out.reshape(1, -1), s0)


def _sortable_key(score):
    bits = lax.bitcast_convert_type(score + 0.0, I32)
    return bits ^ ((bits >> 31) & 0x7FFFFFFF)


def _select_threshold(key_ref, n_chunks, ktarget, pos0, rows, unroll=False):
    ck = key_ref.shape[-1]

    def count(pred):
        def body(c, cnt):
            blk = key_ref[c]
            pos = pos0(c) + lax.broadcasted_iota(I32, (rows, ck), 1)
            m = jnp.where(pred(blk, pos), 1, 0)
            parts = [m[:, n * LANES:(n + 1) * LANES] for n in range(ck // LANES)]
            while len(parts) > 1:
                parts = [parts[p] + parts[p + 1] for p in range(0, len(parts), 2)]
            return cnt + parts[0]
        cnt = lax.fori_loop(0, n_chunks, body, jnp.zeros((rows, LANES), I32), unroll=unroll)
        return jnp.sum(cnt, axis=-1, keepdims=True)

    def count_ge(thr):
        return count(lambda blk, pos: blk >= thr)

    thr = jnp.where(count_ge(jnp.zeros((rows, 1), I32)) >= ktarget, 0, INT_MIN).astype(I32)

    def bit_step(it, thr):
        cand = thr | jnp.left_shift(jnp.int32(1), 30 - it)
        return jnp.where(count_ge(cand) >= ktarget, cand, thr)

    thr = lax.fori_loop(0, 31, bit_step, thr)
    n_gt = count(lambda blk, pos: blk > thr)
    n_ge = count_ge(thr)
    need = ktarget - n_gt
    extra = jnp.max(n_ge - n_gt - need)

    def tie_cut():
        def step(_, lohi):
            lo, hi = lohi
            mid = lo + ((hi - lo) >> 1)
            c = count(lambda blk, pos: (blk == thr) & (pos <= mid))
            ok = c >= need
            return jnp.where(ok, lo, mid), jnp.where(ok, mid, hi)
        lo = jnp.full((rows, 1), -1, I32)
        hi = jnp.full((rows, 1), 2 ** 30, I32)
        _, hi = lax.fori_loop(0, 31, step, (lo, hi))
        return hi

    cut = lax.cond(extra > 0, tie_cut, lambda: jnp.full((rows, 1), 2 ** 30, I32))
    return thr, cut


ROW_BLOCK = 256
SWEEP_UNROLL = 8


def _bit_transpose32(words):
    a = list(words)
    j, m = 16, 0x0000FFFF
    while j:
        for k in range(32):
            if k & j == 0:
                t = (a[k] ^ lax.shift_right_logical(a[k + j], jnp.int32(j))) & m
                a[k] = a[k] ^ t
                a[k + j] = a[k + j] ^ (t << j)
        j >>= 1
        m = m ^ (m << j)
    return a


def _idx_kernel(iqt_ref, wt_ref, ki_ref, bias_ref, p_ref, c_ref, g_ref, *, tq, ck, k_sel):
    i = pl.program_id(0)
    T = ki_ref.shape[0]
    RB = ROW_BLOCK
    n_all = T // ck
    n_chunks = ((i + 1) * tq + ck - 1) // ck
    n_rb = n_chunks * (ck // RB)
    t_pos = i * tq + lax.broadcasted_iota(I32, (1, tq), 1)
    w = wt_ref[...]

    @pl.when(i == 0)
    def _():
        c_ref[...] = jnp.zeros(c_ref.shape, I32)
        g_ref[...] = jnp.zeros(g_ref.shape, I32)
        p_ref[...] = jnp.zeros(p_ref.shape, I32)

    def score_chunk(c, carry):
        r0 = pl.multiple_of(c * ck, ck)
        kc = ki_ref[pl.ds(r0, ck), :]
        acc = jnp.zeros((ck, tq), F32)
        for h in range(IDX_HEADS):
            d = jnp.dot(kc, iqt_ref[h * IDX_DIM:(h + 1) * IDX_DIM, :], preferred_element_type=F32)
            acc = acc + jnp.maximum(d, 0.0) * w[h:h + 1, :]
        s_pos = r0 + lax.broadcasted_iota(I32, (ck, tq), 0)
        key = _sortable_key(jnp.where(s_pos <= t_pos, acc, -jnp.inf)) ^ jnp.int32(INT_MIN)
        for half in range(ck // RB):
            rb = c * (ck // RB) + half
            base = half * RB
            words = [key[base + 8 * (31 - x):base + 8 * (31 - x) + 8, :] for x in range(32)]
            planes = _bit_transpose32(words)
            for b in range(32):
                p_ref[b, rb] = planes[31 - b]
            c_ref[rb] = jnp.full((8, tq), -1, I32)
            g_ref[rb] = jnp.zeros((8, tq), I32)
        return carry

    lax.fori_loop(0, n_chunks, score_chunk, 0)

    def lane_total(cnt):
        return jnp.sum(cnt, axis=0, keepdims=True)

    def sweep_row_blocks(per_block):
        unroll = math.gcd(SWEEP_UNROLL, T // RB)

        def body(it, cnt):
            for u in range(unroll):
                cnt = cnt + per_block(it * unroll + u)
            return cnt
        n_it = (n_rb + unroll - 1) // unroll
        return lane_total(lax.fori_loop(0, n_it, body, jnp.zeros((8, tq), I32)))

    def decide(ones_count, k_rem):
        keep_ones = ones_count >= k_rem
        return jnp.where(keep_ones, k_rem, k_rem - ones_count), jnp.where(keep_ones, 0, -1)

    def apply_decision(s, b_prev, flip):
        cs = c_ref[s]
        pp = p_ref[b_prev, s]
        g_ref[s] = g_ref[s] | (cs & pp & flip)
        cs = cs & (pp ^ flip)
        c_ref[s] = cs
        return cs

    top = sweep_row_blocks(lambda s: lax.population_count(c_ref[s] & p_ref[31, s]))
    k_rem, flip = decide(top, jnp.minimum(t_pos + 1, k_sel))

    def sweep(it, state):
        k_rem, flip = state
        b = 30 - it
        ones = sweep_row_blocks(lambda s: lax.population_count(apply_decision(s, b + 1, flip) & p_ref[b, s]))
        return decide(ones, k_rem)

    k_rem, flip = lax.fori_loop(0, 31, sweep, (k_rem, flip))
    n_tie = sweep_row_blocks(lambda s: lax.population_count(apply_decision(s, 0, flip)))
    extra = jnp.max(n_tie - k_rem)

    def write_bias(sel_fn):
        def body(s, carry):
            sel = sel_fn(s)
            for jp in range(16):
                rows = jnp.concatenate([(sel >> (2 * jp)) & 1, (sel >> (2 * jp + 1)) & 1], axis=0)
                r0 = pl.multiple_of(s * RB + 16 * jp, 16)
                bias_ref[0, pl.ds(r0, 16), :] = jnp.where(rows != 0, 0.0, MASKED).astype(bias_ref.dtype)
            return carry
        lax.fori_loop(0, n_rb, body, 0)

    def no_surplus_ties():
        write_bias(lambda s: g_ref[s] | c_ref[s])

    def surplus_ties():
        sub = lax.broadcasted_iota(I32, (8, tq), 0)

        def upto(s, cut):
            nb = jnp.clip(((cut - s * RB - sub) >> 3) + 1, 0, 32)
            return jnp.where(nb >= 32, -1, (jnp.int32(1) << jnp.minimum(nb, 31)) - 1)

        def step(_, lohi):
            lo, hi = lohi
            mid = lo + ((hi - lo) >> 1)
            c = sweep_row_blocks(lambda s: lax.population_count(c_ref[s] & upto(s, mid)))
            ok = c >= k_rem
            return jnp.where(ok, lo, mid), jnp.where(ok, mid, hi)

        _, cut = lax.fori_loop(0, 15, step, (jnp.full((1, tq), -1, I32), jnp.full((1, tq), T - 1, I32)))
        write_bias(lambda s: g_ref[s] | (c_ref[s] & upto(s, cut)))

    lax.cond(extra > 0, surplus_ties, no_surplus_ties)

    def write_rest(c, carry):
        r0 = pl.multiple_of(c * ck, ck)
        bias_ref[0, pl.ds(r0, ck), :] = jnp.full((ck, tq), MASKED, bias_ref.dtype)
        return carry

    lax.fori_loop(n_chunks, n_all, write_rest, 0)


def idx_select(ct, wt, ki, T, tq, ck):
    k_sel = min(TOPK_MAX, T // 4)
    return pl.pallas_call(
        functools.partial(_idx_kernel, tq=tq, ck=ck, k_sel=k_sel),
        grid=(T // tq,),
        in_specs=[pl.BlockSpec((IDX_HEADS * IDX_DIM, tq), lambda i: (CT_IQ // (IDX_HEADS * IDX_DIM), i)),
                  pl.BlockSpec((IDX_HEADS, tq), lambda i: (0, i)),
                  pl.BlockSpec((T, IDX_DIM), lambda i: (0, 0))],
        out_specs=pl.BlockSpec((1, T, tq), lambda i: (i, 0, 0)),
        out_shape=jax.ShapeDtypeStruct((T // tq, T, tq), BF16),
        scratch_shapes=[pltpu.VMEM((32, T // ROW_BLOCK, 8, tq), I32),
                        pltpu.VMEM((T // ROW_BLOCK, 8, tq), I32), pltpu.VMEM((T // ROW_BLOCK, 8, tq), I32)],
        compiler_params=pltpu.CompilerParams(dimension_semantics=("arbitrary",), vmem_limit_bytes=IDX_VMEM_LIMIT),
        name="idx_select",
    )(ct, wt, ki)


DSA_DIM_AUG = DSA_DIM + 16
LOG2E = 1.4426950408889634
QK_AHEAD = 2


def _attn_kernel(ii_ref, jj_ref, qt_ref, k_ref, vt_ref, b_ref, o_ref, m_ref, l_ref, acc_ref, *, tq, tk):
    H, D, DA = DSA_HEADS, DSA_DIM, DSA_DIM_AUG
    s_id = pl.program_id(0)
    i, j = ii_ref[s_id], jj_ref[s_id]

    @pl.when(j == 0)
    def _():
        m_ref[...] = jnp.full(m_ref.shape, M_INIT, F32)
        l_ref[...] = jnp.zeros(l_ref.shape, F32)
        acc_ref[...] = jnp.zeros(acc_ref.shape, F32)

    bias = jnp.concatenate([b_ref[n] for n in range(b_ref.shape[0])], axis=1).astype(F32)

    def qk(h):
        return jnp.dot(k_ref[:, h * D:(h + 1) * D], qt_ref[h * D:(h + 1) * D, :], preferred_element_type=F32)

    pending = [qk(h) for h in range(QK_AHEAD)]
    for h in range(H):
        s = pending.pop(0)
        if h + QK_AHEAD < H:
            pending.append(qk(h + QK_AHEAD))
        s = s * (D ** -0.5 * LOG2E) + bias
        m_prev = m_ref[h:h + 1, :]
        m_new = jnp.maximum(m_prev, jnp.max(s, axis=0, keepdims=True))
        alpha = jnp.exp2(m_prev - m_new)
        pr = jnp.exp2(s - m_new).astype(BF16)
        pv = jnp.dot(vt_ref[h * DA:(h + 1) * DA, :], pr, preferred_element_type=F32)
        acc_ref[h * D:(h + 1) * D, :] = alpha * acc_ref[h * D:(h + 1) * D, :] + pv[:D]
        l_ref[h:h + 1, :] = alpha * l_ref[h:h + 1, :] + pv[D:D + 1]
        m_ref[h:h + 1, :] = m_new

    @pl.when(j == (i * tq + tq - 1) // tk)
    def _():
        for h in range(H):
            o_ref[h * D:(h + 1) * D, :] = (acc_ref[h * D:(h + 1) * D, :] / l_ref[h:h + 1, :]).astype(o_ref.dtype)


def attention(ct, k, vt_aug, bias, T, tq, tk):
    H, D, DA = DSA_HEADS, DSA_DIM, DSA_DIM_AUG
    tqb = bias.shape[2]
    pairs = [(i, j) for i in range(T // tq) for j in range((i * tq + tq - 1) // tk + 1)]
    ii = jnp.asarray([p[0] for p in pairs], I32)
    jj = jnp.asarray([p[1] for p in pairs], I32)
    grid_spec = pltpu.PrefetchScalarGridSpec(
        num_scalar_prefetch=2,
        grid=(len(pairs),),
        in_specs=[pl.BlockSpec((H * D, tq), lambda s, ii, jj: (CT_AQ // (H * D), ii[s])),
                  pl.BlockSpec((tk, H * D), lambda s, ii, jj: (jj[s], 0)),
                  pl.BlockSpec((H * DA, tk), lambda s, ii, jj: (0, jj[s])),
                  pl.BlockSpec((tq // tqb, tk, tqb), lambda s, ii, jj: (ii[s], jj[s], 0))],
        out_specs=pl.BlockSpec((H * D, tq), lambda s, ii, jj: (0, ii[s])),
        scratch_shapes=[pltpu.VMEM((H, tq), F32), pltpu.VMEM((H, tq), F32), pltpu.VMEM((H * D, tq), F32)],
    )
    return pl.pallas_call(
        functools.partial(_attn_kernel, tq=tq, tk=tk),
        grid_spec=grid_spec,
        out_shape=jax.ShapeDtypeStruct((H * D, T), BF16),
        compiler_params=_cparams(("arbitrary",)),
        name="attention",
    )(ii, jj, ct, k, vt_aug, bias)


def _idx_sample_kernel(pt_ref, q_ref, w_ref, kn_ref, cache_ref, bias_ref, kbuf_ref, key_ref, sem, *,
                       n_pages, ck, k_sel, t_dec):
    b = pl.program_id(0)
    n_chunks = key_ref.shape[0]

    def page_copy(pg):
        return pltpu.make_async_copy(cache_ref.at[pt_ref[b, pg]],
                                     kbuf_ref.at[:, pl.ds(pl.multiple_of(pg * PAGE, PAGE), PAGE)], sem)

    def start(pg, c):
        page_copy(pg).start()
        return c

    lax.fori_loop(0, n_pages, start, 0)
    kbuf_ref[:, n_pages * PAGE:(n_pages + 1) * PAGE] = kn_ref[0]
    kbuf_ref[:, (n_pages + 1) * PAGE:] = jnp.zeros((IDX_DIM, ck - PAGE), F32)

    def wait(pg, c):
        page_copy(pg).wait()
        return c

    lax.fori_loop(0, n_pages, wait, 0)

    q = q_ref[0]
    wcol = w_ref[0]
    t_pos = n_pages * PAGE + lax.broadcasted_iota(I32, (t_dec, 1), 0)

    def score_chunk(c, carry):
        kc = kbuf_ref[:, pl.ds(pl.multiple_of(c * ck, ck), ck)].astype(BF16)
        d = jnp.dot(q, kc, preferred_element_type=F32)
        d = jnp.maximum(d, 0.0) * wcol
        sc = d[0:t_dec]
        for h in range(1, IDX_HEADS):
            sc = sc + d[h * t_dec:(h + 1) * t_dec]
        s_pos = c * ck + lax.broadcasted_iota(I32, (t_dec, ck), 1)
        key_ref[c] = _sortable_key(jnp.where(s_pos <= t_pos, sc, -jnp.inf))
        return carry

    lax.fori_loop(0, n_chunks, score_chunk, 0)

    ktarget = jnp.full((t_dec, 1), k_sel, I32)
    thr, cut = _select_threshold(key_ref, n_chunks, ktarget, lambda c: c * ck, t_dec, unroll=True)

    def write_sel(c, carry):
        blk = key_ref[c]
        pos = c * ck + lax.broadcasted_iota(I32, (t_dec, ck), 1)
        sel = (blk > thr) | ((blk == thr) & (pos <= cut))
        bias_ref[0, c] = jnp.where(sel, 0.0, MASKED)
        return carry

    lax.fori_loop(0, n_chunks, write_sel, 0)


def idx_select_sample(iq, iw, kn, cache_idx_kt, page_table, B, t_dec, ck):
    n_pages = page_table.shape[1]
    past = n_pages * PAGE
    k_sel = min(TOPK_MAX, (past + t_dec) // 4)
    n_chunks = past // ck + 1
    iw = iw.reshape(B, t_dec, IDX_HEADS)
    iw = (iw * (IDX_HEADS ** -0.5 * IDX_DIM ** -0.5)).transpose(0, 2, 1).reshape(B, IDX_HEADS * t_dec, 1)
    kn = jnp.pad(kn.reshape(B, t_dec, IDX_DIM).transpose(0, 2, 1), ((0, 0), (0, 0), (0, PAGE - t_dec)))
    grid_spec = pltpu.PrefetchScalarGridSpec(
        num_scalar_prefetch=1,
        grid=(B,),
        in_specs=[pl.BlockSpec((1, IDX_HEADS * t_dec, IDX_DIM), lambda b, pt: (b, 0, 0)),
                  pl.BlockSpec((1, IDX_HEADS * t_dec, 1), lambda b, pt: (b, 0, 0)),
                  pl.BlockSpec((1, IDX_DIM, PAGE), lambda b, pt: (b, 0, 0)),
                  pl.BlockSpec(memory_space=pl.ANY)],
        out_specs=pl.BlockSpec((1, n_chunks, t_dec, ck), lambda b, pt: (b, 0, 0, 0)),
        scratch_shapes=[pltpu.VMEM((IDX_DIM, past + ck), F32),
                        pltpu.VMEM((n_chunks, t_dec, ck), I32),
                        pltpu.SemaphoreType.DMA(())],
    )
    return pl.pallas_call(
        functools.partial(_idx_sample_kernel, n_pages=n_pages, ck=ck, k_sel=k_sel, t_dec=t_dec),
        grid_spec=grid_spec,
        out_shape=jax.ShapeDtypeStruct((B, n_chunks, t_dec, ck), F32),
        compiler_params=_cparams(("arbitrary",)),
        name="idx_select_sample",
    )(page_table, iq, iw, kn, cache_idx_kt)


KEY_GROUP = 64


def _attn_sample_kernel(pt_ref, q_ref, kn_ref, vn_ref, b_ref, hm_ref, ex_ref, *refs, pg, t_dec):
    H, D = DSA_HEADS, DSA_DIM
    k_refs, v_refs = refs[:pg], refs[pg:2 * pg]
    o_ref, m_ref, l_ref, acc_ref = refs[2 * pg:]
    j = pl.program_id(1)
    nj = pl.num_programs(1)
    n_groups = pg * PAGE // KEY_GROUP

    @pl.when(j == 0)
    def _():
        m_ref[...] = jnp.full(m_ref.shape, M_INIT, F32)
        l_ref[...] = jnp.zeros(l_ref.shape, F32)
        acc_ref[...] = jnp.zeros(acc_ref.shape, F32)

    def update(kc, vc):
        b = b_ref[0, 0]
        by_group = jnp.concatenate([b[:, g * KEY_GROUP:(g + 1) * KEY_GROUP] for g in range(n_groups)], axis=0)
        wide = jnp.dot(by_group.astype(BF16), ex_ref[...], preferred_element_type=F32)
        hm = hm_ref[...]
        bias = jnp.concatenate(
            [jnp.concatenate([wide[g * t_dec:(g + 1) * t_dec]] * H, axis=0) + hm for g in range(n_groups)], axis=1)
        s = lax.dot_general(q_ref[0], kc, (((1,), (1,)), ((), ())), preferred_element_type=F32)
        s = s * (D ** -0.5) + bias
        m_prev = m_ref[...]
        m_new = jnp.maximum(m_prev, jnp.max(s, axis=-1, keepdims=True))
        alpha = jnp.exp(m_prev - m_new)
        pr = jnp.exp(s - m_new)
        l_ref[...] = alpha * l_ref[...] + jnp.sum(pr, axis=-1, keepdims=True)
        acc_ref[...] = alpha * acc_ref[...] + jnp.dot(pr.astype(BF16), vc, preferred_element_type=F32)
        m_ref[...] = m_new

    @pl.when(j < nj - 1)
    def _():
        kc = jnp.concatenate([r[0] for r in k_refs], axis=0).astype(BF16)
        vc = jnp.concatenate([r[0] for r in v_refs], axis=0).astype(BF16)
        update(kc, vc)

    @pl.when(j == nj - 1)
    def _():
        pad = jnp.zeros(((pg * PAGE - t_dec) * H, D), F32)
        kc = jnp.concatenate([kn_ref[0], pad], axis=0).astype(BF16)
        vc = jnp.concatenate([vn_ref[0], pad], axis=0).astype(BF16)
        update(kc, vc)
        out = acc_ref[...] / l_ref[...]
        for h in range(H):
            o_ref[0, :, h * D:(h + 1) * D] = out[h * t_dec:(h + 1) * t_dec].astype(o_ref.dtype)


def attention_sample(q, kn, vn, cache_k, cache_v, page_table, bias, B, t_dec, pg):
    H, D = DSA_HEADS, DSA_DIM
    n_pages = page_table.shape[1]
    nj = n_pages // pg + 1
    gl = KEY_GROUP * H
    q_head = jnp.arange(H * t_dec, dtype=I32)[:, None] // t_dec
    k_head = jnp.arange(gl, dtype=I32)[None, :] % H
    head_mask = jnp.where(q_head == k_head, 0.0, MASKED).astype(F32)
    expand = (jnp.arange(KEY_GROUP, dtype=I32)[:, None] == jnp.arange(gl, dtype=I32)[None, :] // H).astype(BF16)

    def page_map(u):
        return lambda b, j, pt: (pt[b, jnp.minimum(j, n_pages // pg - 1) * pg + u], 0, 0)

    page_specs = [pl.BlockSpec((1, PAGE * H, D), page_map(u)) for u in range(pg)]
    grid_spec = pltpu.PrefetchScalarGridSpec(
        num_scalar_prefetch=1,
        grid=(B, nj),
        in_specs=[pl.BlockSpec((1, H * t_dec, D), lambda b, j, pt: (b, 0, 0)),
                  pl.BlockSpec((1, t_dec * H, D), lambda b, j, pt: (b, 0, 0)),
                  pl.BlockSpec((1, t_dec * H, D), lambda b, j, pt: (b, 0, 0)),
                  pl.BlockSpec((1, 1, t_dec, pg * PAGE), lambda b, j, pt: (b, j, 0, 0)),
                  pl.BlockSpec((H * t_dec, gl), lambda b, j, pt: (0, 0)),
                  pl.BlockSpec((KEY_GROUP, gl), lambda b, j, pt: (0, 0))]
        + page_specs + page_specs,
        out_specs=pl.BlockSpec((1, t_dec, H * D), lambda b, j, pt: (b, 0, 0)),
        scratch_shapes=[pltpu.VMEM((H * t_dec, 1), F32), pltpu.VMEM((H * t_dec, 1), F32),
                        pltpu.VMEM((H * t_dec, D), F32)],
    )
    return pl.pallas_call(
        functools.partial(_attn_sample_kernel, pg=pg, t_dec=t_dec),
        grid_spec=grid_spec,
        out_shape=jax.ShapeDtypeStruct((B, t_dec, H * D), F32),
        compiler_params=_cparams(("parallel", "arbitrary")),
        name="attention_sample",
    )(page_table, q, kn, vn, bias, head_mask, expand, *([cache_k] * pg), *([cache_v] * pg))


def _out_proj_kernel(x_ref, a_ref, b_ref, wa_ref, wb_ref, o_ref):
    o_ref[...] = (x_ref[...] + jnp.dot(a_ref[...], wa_ref[...], preferred_element_type=F32)
                  + jnp.dot(b_ref[...], wb_ref[...], preferred_element_type=F32))


def out_proj(x, a, b, wa, wb, tm, tn):
    M, D = x.shape
    Ka, Kb = a.shape[1], b.shape[1]
    return pl.pallas_call(
        _out_proj_kernel,
        grid=(M // tm, D // tn),
        in_specs=[pl.BlockSpec((tm, tn), lambda i, j: (i, j)),
                  pl.BlockSpec((tm, Ka), lambda i, j: (i, 0)),
                  pl.BlockSpec((tm, Kb), lambda i, j: (i, 0)),
                  pl.BlockSpec((Ka, tn), lambda i, j: (0, j)),
                  pl.BlockSpec((Kb, tn), lambda i, j: (0, j))],
        out_specs=pl.BlockSpec((tm, tn), lambda i, j: (i, j)),
        out_shape=jax.ShapeDtypeStruct((M, D), F32),
        compiler_params=_cparams(("parallel", "arbitrary")),
        name="out_proj",
    )(x, a, b, wa, wb)


def _ffn_kernel(x_ref, g_ref, wu_ref, wd_ref, gf_ref, o_ref, h_ref, acc_ref):
    f = pl.program_id(1)

    @pl.when(f == 0)
    def _():
        x = x_ref[...]
        ms = jnp.mean(x * x, axis=-1, keepdims=True)
        h_ref[...] = (x * lax.rsqrt(ms + EPS) * g_ref[...]).astype(BF16)
        acc_ref[...] = jnp.zeros(acc_ref.shape, F32)

    u = jnp.dot(h_ref[...], wu_ref[...], preferred_element_type=F32)
    u = jnp.square(jnp.maximum(u, 0.0)).astype(BF16)
    acc_ref[...] += jnp.dot(u, wd_ref[...], preferred_element_type=F32)

    @pl.when(f == pl.num_programs(1) - 1)
    def _():
        x2 = x_ref[...] + acc_ref[...]
        ms = jnp.mean(x2 * x2, axis=-1, keepdims=True)
        o_ref[...] = x2 * lax.rsqrt(ms + EPS) * gf_ref[...]


def ffn(x, g_ffn, w_up, w_down, g_final, tm, tf):
    M, D = x.shape
    Fd = w_up.shape[1]
    return pl.pallas_call(
        _ffn_kernel,
        grid=(M // tm, Fd // tf),
        in_specs=[pl.BlockSpec((tm, D), lambda i, f: (i, 0)),
                  pl.BlockSpec((1, D), lambda i, f: (0, 0)),
                  pl.BlockSpec((D, tf), lambda i, f: (0, f)),
                  pl.BlockSpec((tf, D), lambda i, f: (f, 0)),
                  pl.BlockSpec((1, D), lambda i, f: (0, 0))],
        out_specs=pl.BlockSpec((tm, D), lambda i, f: (i, 0)),
        out_shape=jax.ShapeDtypeStruct((M, D), F32),
        scratch_shapes=[pltpu.VMEM((tm, D), BF16), pltpu.VMEM((tm, D), F32)],
        compiler_params=_cparams(("parallel", "arbitrary")),
        name="ffn",
    )(x, g_ffn.reshape(1, D), w_up, w_down, g_final.reshape(1, D))


def _pack_w_in(w_in):
    D = w_in.shape[0]
    gq, gk, gv, gg, ga, aq, ak, av, iq, iw, ik = jnp.split(
        w_in, [512, 1024, 2048, 3072, 3088, 4112, 5136, 6160, 7184, 7200], axis=1)
    pad = jnp.zeros((D, PA_WIDTH - C_SMALL - 96), w_in.dtype)
    w_a = jnp.concatenate([gq, gk, gv, gg, ik, ga, iw, pad], axis=1).astype(BF16)
    w_b = jnp.concatenate([ak, av], axis=1).astype(BF16)
    w_ct = jnp.concatenate([aq, av, iq], axis=1).T.astype(BF16)
    return w_a, w_b, w_ct


def _tile(n, pref):
    return pref if n % pref == 0 else n


def kernel(x_prompt, x_sample, cache_k, cache_v, cache_idx_k, state_gla, page_table,
           g_mix, w_in, w_gate_up, b_gate, g_gla_out, w_o, g_ffn, w_up, w_down, g_final):
    depth = w_in.shape[0]
    Bp, Tp, D = x_prompt.shape
    Bs, Ts, _ = x_sample.shape
    assert Bp == 1 and depth == 1
    H, Dh = DSA_HEADS, DSA_DIM
    n_pool = cache_k.shape[1]

    xp = x_prompt.reshape(Bp * Tp, D)
    xs = x_sample.reshape(Bs * Ts, D)
    l = 0
    w_a, w_b, w_ct = _pack_w_in(w_in[l])
    wo_a = w_o[l, :GLA_HEADS * GLA_DV].astype(BF16)
    wo_b = w_o[l, GLA_HEADS * GLA_DV:].astype(BF16)
    wu = w_up[l].astype(BF16)
    wd = w_down[l].astype(BF16)
    idx_scale = IDX_HEADS ** -0.5 * IDX_DIM ** -0.5

    tm = _tile(Tp, 512)
    pa_p = norm_matmul(xp, g_mix[l], w_a, tm, PA_WIDTH // 2)
    k3_p, v3_p, kb_p = norm_matmul_heads(xp, g_mix[l], w_b, tm)
    ct_p = norm_matmul_t(xp, g_mix[l], w_ct, tm, 1024)
    s0 = jnp.zeros((Bp, GLA_HEADS, GLA_DK, GLA_DV), F32)
    o_gla_p, s_fin_p = gla(pa_p, w_gate_up[l], b_gate[l], g_gla_out[l], s0, Bp, Tp, _tile(Tp, 128), 16)
    ik_p = pa_p[:, C_SMALL + SM_IK:C_SMALL + SM_IK + IDX_DIM]
    wt = (pa_p[:, C_SMALL + SM_IW:C_SMALL + SM_IW + IDX_HEADS] * idx_scale).T
    bias = idx_select(ct_p, wt, ik_p.astype(BF16), Tp, _tile(Tp, 256), _tile(Tp, 512))
    vt = ct_p[CT_AV:CT_AV + H * Dh].reshape(H, Dh, Tp)
    vt_aug = jnp.concatenate([vt, jnp.ones((H, DSA_DIM_AUG - Dh, Tp), BF16)], axis=1).reshape(H * DSA_DIM_AUG, Tp)
    o_dsa_p = attention(ct_p, kb_p, vt_aug, bias, Tp, tm, _tile(Tp, 1024)).T
    x1p = out_proj(xp, o_gla_p, o_dsa_p, wo_a, wo_b, tm, 1024)
    y_p = ffn(x1p, g_ffn[l], wu, wd, g_final, tm, 1024)

    Ms = Bs * Ts
    pa_s = norm_matmul(xs, g_mix[l], w_a, Ms, PA_WIDTH // 2)
    k3_s, v3_s, _ = norm_matmul_heads(xs, g_mix[l], w_b, Ms)
    ct_s = norm_matmul_t(xs, g_mix[l], w_ct, Ms, 1024)
    o_gla_s, s_fin_s = gla(pa_s, w_gate_up[l], b_gate[l], g_gla_out[l], state_gla[l], Bs, Ts, Ts, Ts)
    ik_s = pa_s[:, C_SMALL + SM_IK:C_SMALL + SM_IK + IDX_DIM]
    iq_s = ct_s[CT_IQ:CT_IQ + IDX_HEADS * IDX_DIM].reshape(IDX_HEADS, IDX_DIM, Bs, Ts)
    iq_s = iq_s.transpose(2, 0, 3, 1).reshape(Bs, IDX_HEADS * Ts, IDX_DIM)
    pg = 8
    bias_s = idx_select_sample(iq_s, pa_s[:, C_SMALL + SM_IW:C_SMALL + SM_IW + IDX_HEADS], ik_s,
                               jnp.swapaxes(cache_idx_k[l], 1, 2), page_table, Bs, Ts, pg * PAGE)
    q_s = ct_s[CT_AQ:CT_AQ + H * Dh].reshape(H, Dh, Bs, Ts).transpose(2, 0, 3, 1).reshape(Bs, H * Ts, Dh)
    o_dsa_s = attention_sample(q_s, k3_s.reshape(Bs, Ts * H, Dh), v3_s.reshape(Bs, Ts * H, Dh),
                               cache_k[l].reshape(n_pool, PAGE * H, Dh), cache_v[l].reshape(n_pool, PAGE * H, Dh),
                               page_table, bias_s, Bs, Ts, pg)
    x1s = out_proj(xs, o_gla_s, o_dsa_s.reshape(Ms, H * Dh).astype(BF16), wo_a, wo_b, Ms, 1024)
    y_s = ffn(x1s, g_ffn[l], wu, wd, g_final, Ms, 512)

    return (y_p.reshape(Bp, Tp, D), y_s.reshape(Bs, Ts, D),
            k3_p.reshape(1, Bp, Tp, H, Dh), v3_p.reshape(1, Bp, Tp, H, Dh),
            ik_p.reshape(1, Bp, Tp, IDX_DIM), s_fin_p[None],
            k3_s.reshape(1, Bs, Ts, H, Dh), v3_s.reshape(1, Bs, Ts, H, Dh),
            ik_s.reshape(1, Bs, Ts, IDX_DIM), s_fin_s[None])
```

```python
import functools
import math

import jax
import jax.numpy as jnp
from jax import lax
from jax.experimental import pallas as pl
from jax.experimental.pallas import tpu as pltpu

F32 = jnp.float32
BF16 = jnp.bfloat16
I32 = jnp.int32

EPS = 1e-6
GATE_TAU = 16.0
GLA_HEADS = 4
GLA_DK = 128
GLA_DV = 256
GATE_RANK = 16
DSA_HEADS = 8
DSA_DIM = 128
IDX_HEADS = 16
IDX_DIM = 64
TOPK_MAX = 256
PAGE = 128

LANES = 128
MASKED = -1e30
M_INIT = -1e29
INT_MIN = -(2 ** 31)
VMEM_LIMIT = 52 * 1024 * 1024
IDX_VMEM_LIMIT = 56 * 1024 * 1024

C_GQ, C_GK, C_GV, C_GG = 0, 512, 1024, 2048
C_SMALL = 3072
SM_IK, SM_GA, SM_IW = 0, 64, 80
PA_WIDTH = 3328
CT_AQ, CT_AV, CT_IQ = 0, 1024, 2048
CT_ROWS = 3072


def _cparams(sem):
    return pltpu.CompilerParams(dimension_semantics=sem, vmem_limit_bytes=VMEM_LIMIT)


def _norm_mm_kernel(x_ref, g_ref, w_ref, o_ref, h_ref):
    @pl.when(pl.program_id(1) == 0)
    def _():
        x = x_ref[...]
        ms = jnp.mean(x * x, axis=-1, keepdims=True)
        h_ref[...] = (x * lax.rsqrt(ms + EPS) * g_ref[...]).astype(BF16)

    o_ref[...] = jnp.dot(h_ref[...], w_ref[...], preferred_element_type=F32)


def norm_matmul(x, g, w, tm, tn):
    M, D = x.shape
    N = w.shape[1]
    return pl.pallas_call(
        _norm_mm_kernel,
        grid=(M // tm, N // tn),
        in_specs=[pl.BlockSpec((tm, D), lambda i, j: (i, 0)),
                  pl.BlockSpec((1, D), lambda i, j: (0, 0)),
                  pl.BlockSpec((D, tn), lambda i, j: (0, j))],
        out_specs=pl.BlockSpec((tm, tn), lambda i, j: (i, j)),
        out_shape=jax.ShapeDtypeStruct((M, N), F32),
        scratch_shapes=[pltpu.VMEM((tm, D), BF16)],
        compiler_params=_cparams(("parallel", "arbitrary")),
        name="norm_matmul",
    )(x, g.reshape(1, D), w)


def _norm_mm_heads_kernel(x_ref, g_ref, w_ref, k3_ref, v3_ref, kb_ref, h_ref):
    H, D = DSA_HEADS, DSA_DIM
    j = pl.program_id(1)

    @pl.when(j == 0)
    def _():
        x = x_ref[...]
        ms = jnp.mean(x * x, axis=-1, keepdims=True)
        h_ref[...] = (x * lax.rsqrt(ms + EPS) * g_ref[...]).astype(BF16)

    r = jnp.dot(h_ref[...], w_ref[...], preferred_element_type=F32)

    @pl.when(j == 0)
    def _():
        kb_ref[...] = r.astype(BF16)
        for h in range(H):
            k3_ref[:, h, :] = r[:, h * D:(h + 1) * D]

    @pl.when(j == 1)
    def _():
        for h in range(H):
            v3_ref[:, h, :] = r[:, h * D:(h + 1) * D]


def norm_matmul_heads(x, g, w, tm):
    M, D = x.shape
    H, Dh = DSA_HEADS, DSA_DIM
    return pl.pallas_call(
        _norm_mm_heads_kernel,
        grid=(M // tm, 2),
        in_specs=[pl.BlockSpec((tm, D), lambda i, j: (i, 0)),
                  pl.BlockSpec((1, D), lambda i, j: (0, 0)),
                  pl.BlockSpec((D, H * Dh), lambda i, j: (0, j))],
        out_specs=[pl.BlockSpec((tm, H, Dh), lambda i, j: (i, 0, 0)),
                   pl.BlockSpec((tm, H, Dh), lambda i, j: (i, 0, 0)),
                   pl.BlockSpec((tm, H * Dh), lambda i, j: (i, 0))],
        out_shape=[jax.ShapeDtypeStruct((M, H, Dh), F32), jax.ShapeDtypeStruct((M, H, Dh), F32),
                   jax.ShapeDtypeStruct((M, H * Dh), BF16)],
        scratch_shapes=[pltpu.VMEM((tm, D), BF16)],
        compiler_params=_cparams(("parallel", "arbitrary")),
        name="norm_matmul_heads",
    )(x, g.reshape(1, D), w)


def _norm_mm_t_kernel(x_ref, g_ref, wt_ref, o_ref, h_ref):
    @pl.when(pl.program_id(1) == 0)
    def _():
        x = x_ref[...]
        ms = jnp.mean(x * x, axis=-1, keepdims=True)
        h_ref[...] = (x * lax.rsqrt(ms + EPS) * g_ref[...]).astype(BF16)

    o_ref[...] = lax.dot_general(wt_ref[...], h_ref[...], (((1,), (1,)), ((), ())),
                                 preferred_element_type=F32).astype(o_ref.dtype)


def norm_matmul_t(x, g, wt, tm, tn):
    M, D = x.shape
    N = wt.shape[0]
    return pl.pallas_call(
        _norm_mm_t_kernel,
        grid=(M // tm, N // tn),
        in_specs=[pl.BlockSpec((tm, D), lambda i, j: (i, 0)),
                  pl.BlockSpec((1, D), lambda i, j: (0, 0)),
                  pl.BlockSpec((tn, D), lambda i, j: (j, 0))],
        out_specs=pl.BlockSpec((tn, tm), lambda i, j: (j, i)),
        out_shape=jax.ShapeDtypeStruct((N, M), BF16),
        scratch_shapes=[pltpu.VMEM((tm, D), BF16)],
        compiler_params=_cparams(("parallel", "arbitrary")),
        name="norm_matmul_t",
    )(x, g.reshape(1, D), wt)


def _gla_kernel(q_ref, k_ref, v_ref, g_ref, sm_ref, wg_ref, bg_ref, go_ref, s0_ref,
                o_ref, sfin_ref, st_ref, b_ref, oacc_ref, *, tb, C):
    H, DK, DV = GLA_HEADS, GLA_DK, GLA_DV
    t = pl.program_id(1)

    @pl.when(t == 0)
    def _():
        for h in range(H):
            st_ref[h] = s0_ref[0, h].T

    a_lr = sm_ref[:, SM_GA:SM_GA + GATE_RANK]
    z = jnp.dot(a_lr.astype(BF16), wg_ref[...], preferred_element_type=F32) + bg_ref[...]
    log_a = (jnp.minimum(z, 0.0) - jnp.log1p(jnp.exp(-jnp.abs(z)))) / GATE_TAU
    row = lax.broadcasted_iota(I32, (tb, H * DK), 0) & (C - 1)
    b = log_a
    sh = 1
    while sh < C:
        b = b + jnp.where(row >= sh, pltpu.roll(b, sh, axis=0), 0.0)
        sh *= 2
    b_ref[...] = b

    jj = lax.broadcasted_iota(I32, (C, 1), 0)

    def chunk(c, carry):
        r0 = pl.multiple_of(c * C, C)
        for h in range(H):
            bq = b_ref[pl.ds(r0, C), h * DK:(h + 1) * DK]
            q = q_ref[pl.ds(r0, C), h * DK:(h + 1) * DK] * (DK ** -0.5)
            k = k_ref[pl.ds(r0, C), h * DK:(h + 1) * DK]
            v = v_ref[pl.ds(r0, C), h * DV:(h + 1) * DV]
            st = st_ref[h]
            qe = q * jnp.exp(bq)
            o = lax.dot_general(qe.astype(BF16), st.astype(BF16), (((1,), (1,)), ((), ())),
                                preferred_element_type=F32)
            rows = []
            for i in range(C):
                n = 8 if (C > 8 and i < 8) else C
                causal = jj[:n] <= i
                d = jnp.where(causal, bq[i:i + 1, :] - bq[:n], 0.0)
                a = jnp.sum(q[i:i + 1, :] * k[:n] * jnp.exp(d), axis=-1, keepdims=True)
                a = jnp.where(causal, a, 0.0)
                rows.append(jnp.sum(a * v[:n], axis=0, keepdims=True))
            o = o + jnp.concatenate(rows, axis=0)
            oacc_ref[pl.ds(r0, C), h * DV:(h + 1) * DV] = o
            b_last = bq[C - 1:C, :]
            ke = k * jnp.exp(b_last - bq)
            kv_t = lax.dot_general(v.astype(BF16), ke.astype(BF16), (((0,), (0,)), ((), ())),
                                   preferred_element_type=F32)
            st_ref[h] = st * jnp.exp(b_last) + kv_t
        return carry

    lax.fori_loop(0, tb // C, chunk, 0)

    gate = g_ref[...]
    gate = gate * (1.0 / (1.0 + jnp.exp(-gate)))
    for h in range(H):
        o = oacc_ref[:, h * DV:(h + 1) * DV]
        ms = jnp.mean(o * o, axis=-1, keepdims=True)
        y = o * lax.rsqrt(ms + EPS) * go_ref[...]
        o_ref[:, h * DV:(h + 1) * DV] = (y * gate[:, h * DV:(h + 1) * DV]).astype(o_ref.dtype)

    @pl.when(t == pl.num_programs(1) - 1)
    def _():
        for h in range(H):
            sfin_ref[0, h] = st_ref[h].T


def gla(p, w_gate_up, b_gate, g_gla_out, s0, B, T, tb, C):
    H, DK, DV = GLA_HEADS, GLA_DK, GLA_DV
    nt = T // tb
    row = lambda b, t: b * nt + t
    return pl.pallas_call(
        functools.partial(_gla_kernel, tb=tb, C=C),
        grid=(B, nt),
        in_specs=[pl.BlockSpec((tb, H * DK), lambda b, t: (row(b, t), C_GQ // (H * DK))),
                  pl.BlockSpec((tb, H * DK), lambda b, t: (row(b, t), C_GK // (H * DK))),
                  pl.BlockSpec((tb, H * DV), lambda b, t: (row(b, t), C_GV // (H * DV))),
                  pl.BlockSpec((tb, H * DV), lambda b, t: (row(b, t), C_GG // (H * DV))),
                  pl.BlockSpec((tb, LANES), lambda b, t: (row(b, t), C_SMALL // LANES)),
                  pl.BlockSpec((GATE_RANK, H * DK), lambda b, t: (0, 0)),
                  pl.BlockSpec((1, H * DK), lambda b, t: (0, 0)),
                  pl.BlockSpec((1, DV), lambda b, t: (0, 0)),
                  pl.BlockSpec((1, H, DK, DV), lambda b, t: (b, 0, 0, 0))],
        out_specs=[pl.BlockSpec((tb, H * DV), lambda b, t: (row(b, t), 0)),
                   pl.BlockSpec((1, H, DK, DV), lambda b, t: (b, 0, 0, 0))],
        out_shape=[jax.ShapeDtypeStruct((B * T, H * DV), BF16),
                   jax.ShapeDtypeStruct((B, H, DK, DV), F32)],
        scratch_shapes=[pltpu.VMEM((H, DV, DK), F32),
                        pltpu.VMEM((tb, H * DK), F32),
                        pltpu.VMEM((tb, H * DV), F32)],
        compiler_params=_cparams(("arbitrary", "arbitrary")),
        name="gla",
    )(p, p, p, p, p, w_gate_up.astype(BF16), b_gate.reshape(1, -1), g_gla_out.reshape(1, -1), s0)


def _sortable_key(score):
    bits = lax.bitcast_convert_type(score + 0.0, I32)
    return bits ^ ((bits >> 31) & 0x7FFFFFFF)


def _select_threshold(key_ref, n_chunks, ktarget, pos0, rows, unroll=False):
    ck = key_ref.shape[-1]

    def count(pred):
        def body(c, cnt):
            blk = key_ref[c]
            pos = pos0(c) + lax.broadcasted_iota(I32, (rows, ck), 1)
            m = jnp.where(pred(blk, pos), 1, 0)
            parts = [m[:, n * LANES:(n + 1) * LANES] for n in range(ck // LANES)]
            while len(parts) > 1:
                parts = [parts[p] + parts[p + 1] for p in range(0, len(parts), 2)]
            return cnt + parts[0]
        cnt = lax.fori_loop(0, n_chunks, body, jnp.zeros((rows, LANES), I32), unroll=unroll)
        return jnp.sum(cnt, axis=-1, keepdims=True)

    def count_ge(thr):
        return count(lambda blk, pos: blk >= thr)

    thr = jnp.where(count_ge(jnp.zeros((rows, 1), I32)) >= ktarget, 0, INT_MIN).astype(I32)

    def bit_step(it, thr):
        cand = thr | jnp.left_shift(jnp.int32(1), 30 - it)
        return jnp.where(count_ge(cand) >= ktarget, cand, thr)

    thr = lax.fori_loop(0, 31, bit_step, thr)
    n_gt = count(lambda blk, pos: blk > thr)
    n_ge = count_ge(thr)
    need = ktarget - n_gt
    extra = jnp.max(n_ge - n_gt - need)

    def tie_cut():
        def step(_, lohi):
            lo, hi = lohi
            mid = lo + ((hi - lo) >> 1)
            c = count(lambda blk, pos: (blk == thr) & (pos <= mid))
            ok = c >= need
            return jnp.where(ok, lo, mid), jnp.where(ok, mid, hi)
        lo = jnp.full((rows, 1), -1, I32)
        hi = jnp.full((rows, 1), 2 ** 30, I32)
        _, hi = lax.fori_loop(0, 31, step, (lo, hi))
        return hi

    cut = lax.cond(extra > 0, tie_cut, lambda: jnp.full((rows, 1), 2 ** 30, I32))
    return thr, cut


ROW_BLOCK = 256
SWEEP_UNROLL = 8


def _bit_transpose32(words):
    a = list(words)
    j, m = 16, 0x0000FFFF
    while j:
        for k in range(32):
            if k & j == 0:
                t = (a[k] ^ lax.shift_right_logical(a[k + j], jnp.int32(j))) & m
                a[k] = a[k] ^ t
                a[k + j] = a[k + j] ^ (t << j)
        j >>= 1
        m = m ^ (m << j)
    return a


def _idx_kernel(iqt_ref, wt_ref, ki_ref, bias_ref, p_ref, c_ref, g_ref, *, tq, ck, k_sel):
    i = pl.program_id(0)
    T = ki_ref.shape[0]
    RB = ROW_BLOCK
    n_all = T // ck
    n_chunks = ((i + 1) * tq + ck - 1) // ck
    n_rb = n_chunks * (ck // RB)
    t_pos = i * tq + lax.broadcasted_iota(I32, (1, tq), 1)
    w = wt_ref[...]

    @pl.when(i == 0)
    def _():
        c_ref[...] = jnp.zeros(c_ref.shape, I32)
        g_ref[...] = jnp.zeros(g_ref.shape, I32)
        p_ref[...] = jnp.zeros(p_ref.shape, I32)

    def score_chunk(c, carry):
        r0 = pl.multiple_of(c * ck, ck)
        kc = ki_ref[pl.ds(r0, ck), :]
        acc = jnp.zeros((ck, tq), F32)
        for h in range(IDX_HEADS):
            d = jnp.dot(kc, iqt_ref[h * IDX_DIM:(h + 1) * IDX_DIM, :], preferred_element_type=F32)
            acc = acc + jnp.maximum(d, 0.0) * w[h:h + 1, :]
        s_pos = r0 + lax.broadcasted_iota(I32, (ck, tq), 0)
        bits = lax.bitcast_convert_type(jnp.where(s_pos <= t_pos, acc, -jnp.inf) + 0.0, I32)
        key = bits ^ ((bits >> 31) | jnp.int32(INT_MIN))
        for half in range(ck // RB):
            rb = c * (ck // RB) + half
            base = half * RB
            words = [key[base + 8 * (31 - x):base + 8 * (31 - x) + 8, :] for x in range(32)]
            planes = _bit_transpose32(words)
            for b in range(32):
                p_ref[b, rb] = planes[31 - b]
            c_ref[rb] = jnp.full((8, tq), -1, I32)
            g_ref[rb] = jnp.zeros((8, tq), I32)
        return carry

    lax.fori_loop(0, n_chunks, score_chunk, 0)

    def lane_total(cnt):
        return jnp.sum(cnt, axis=0, keepdims=True)

    def sweep_row_blocks(per_block):
        unroll = math.gcd(SWEEP_UNROLL, T // RB)

        def body(it, cnt):
            for u in range(unroll):
                cnt = cnt + per_block(it * unroll + u)
            return cnt
        n_it = (n_rb + unroll - 1) // unroll
        return lane_total(lax.fori_loop(0, n_it, body, jnp.zeros((8, tq), I32)))

    def decide(ones_count, k_rem):
        keep_ones = ones_count >= k_rem
        return jnp.where(keep_ones, k_rem, k_rem - ones_count), jnp.where(keep_ones, 0, -1)

    def apply_decision(s, b_prev, flip):
        cs = c_ref[s]
        pp = p_ref[b_prev, s]
        g_ref[s] = g_ref[s] | (cs & pp & flip)
        cs = cs & (pp ^ flip)
        c_ref[s] = cs
        return cs

    top = sweep_row_blocks(lambda s: lax.population_count(c_ref[s] & p_ref[31, s]))
    k_rem, flip = decide(top, jnp.minimum(t_pos + 1, k_sel))

    def sweep(it, state):
        k_rem, flip = state
        b = 30 - it
        ones = sweep_row_blocks(lambda s: lax.population_count(apply_decision(s, b + 1, flip) & p_ref[b, s]))
        return decide(ones, k_rem)

    k_rem, flip = lax.fori_loop(0, 31, sweep, (k_rem, flip))
    n_tie = sweep_row_blocks(lambda s: lax.population_count(apply_decision(s, 0, flip)))
    extra = jnp.max(n_tie - k_rem)

    def write_bias(sel_fn):
        def body(s, carry):
            sel = sel_fn(s)
            for jp in range(16):
                rows = jnp.concatenate([(sel >> (2 * jp)) & 1, (sel >> (2 * jp + 1)) & 1], axis=0)
                r0 = pl.multiple_of(s * RB + 16 * jp, 16)
                bias_ref[0, pl.ds(r0, 16), :] = jnp.where(rows != 0, 0.0, MASKED).astype(bias_ref.dtype)
            return carry
        lax.fori_loop(0, n_rb, body, 0)

    def no_surplus_ties():
        write_bias(lambda s: g_ref[s] | c_ref[s])

    def surplus_ties():
        sub = lax.broadcasted_iota(I32, (8, tq), 0)

        def upto(s, cut):
            nb = jnp.clip(((cut - s * RB - sub) >> 3) + 1, 0, 32)
            return jnp.where(nb >= 32, -1, (jnp.int32(1) << jnp.minimum(nb, 31)) - 1)

        def step(_, lohi):
            lo, hi = lohi
            mid = lo + ((hi - lo) >> 1)
            c = sweep_row_blocks(lambda s: lax.population_count(c_ref[s] & upto(s, mid)))
            ok = c >= k_rem
            return jnp.where(ok, lo, mid), jnp.where(ok, mid, hi)

        _, cut = lax.fori_loop(0, 15, step, (jnp.full((1, tq), -1, I32), jnp.full((1, tq), T - 1, I32)))
        write_bias(lambda s: g_ref[s] | (c_ref[s] & upto(s, cut)))

    lax.cond(extra > 0, surplus_ties, no_surplus_ties)

    def write_rest(c, carry):
        r0 = pl.multiple_of(c * ck, ck)
        bias_ref[0, pl.ds(r0, ck), :] = jnp.full((ck, tq), MASKED, bias_ref.dtype)
        return carry

    lax.fori_loop(n_chunks, n_all, write_rest, 0)


def idx_select(ct, wt, ki, T, tq, ck):
    k_sel = min(TOPK_MAX, T // 4)
    return pl.pallas_call(
        functools.partial(_idx_kernel, tq=tq, ck=ck, k_sel=k_sel),
        grid=(T // tq,),
        in_specs=[pl.BlockSpec((IDX_HEADS * IDX_DIM, tq), lambda i: (CT_IQ // (IDX_HEADS * IDX_DIM), i)),
                  pl.BlockSpec((IDX_HEADS, tq), lambda i: (0, i)),
                  pl.BlockSpec((T, IDX_DIM), lambda i: (0, 0))],
        out_specs=pl.BlockSpec((1, T, tq), lambda i: (i, 0, 0)),
        out_shape=jax.ShapeDtypeStruct((T // tq, T, tq), BF16),
        scratch_shapes=[pltpu.VMEM((32, T // ROW_BLOCK, 8, tq), I32),
                        pltpu.VMEM((T // ROW_BLOCK, 8, tq), I32), pltpu.VMEM((T // ROW_BLOCK, 8, tq), I32)],
        compiler_params=pltpu.CompilerParams(dimension_semantics=("arbitrary",), vmem_limit_bytes=IDX_VMEM_LIMIT),
        name="idx_select",
    )(ct, wt, ki)


DSA_DIM_AUG = DSA_DIM + 16
LOG2E = 1.4426950408889634
QK_AHEAD = 2


def _attn_kernel(ii_ref, jj_ref, qt_ref, k_ref, vt_ref, b_ref, o_ref, m_ref, l_ref, acc_ref, *, tq, tk):
    H, D, DA = DSA_HEADS, DSA_DIM, DSA_DIM_AUG
    s_id = pl.program_id(0)
    i, j = ii_ref[s_id], jj_ref[s_id]

    @pl.when(j == 0)
    def _():
        m_ref[...] = jnp.full(m_ref.shape, M_INIT, F32)
        l_ref[...] = jnp.zeros(l_ref.shape, F32)
        acc_ref[...] = jnp.zeros(acc_ref.shape, F32)

    bias = jnp.concatenate([b_ref[n] for n in range(b_ref.shape[0])], axis=1).astype(F32)

    def qk(h):
        return jnp.dot(k_ref[:, h * D:(h + 1) * D], qt_ref[h * D:(h + 1) * D, :], preferred_element_type=F32)

    pending = [qk(h) for h in range(QK_AHEAD)]
    for h in range(H):
        s = pending.pop(0)
        if h + QK_AHEAD < H:
            pending.append(qk(h + QK_AHEAD))
        s = s * (D ** -0.5 * LOG2E) + bias
        m_prev = m_ref[h:h + 1, :]
        m_new = jnp.maximum(m_prev, jnp.max(s, axis=0, keepdims=True))
        alpha = jnp.exp2(m_prev - m_new)
        pr = jnp.exp2((s - m_new).astype(BF16))
        pv = jnp.dot(vt_ref[h * DA:(h + 1) * DA, :], pr, preferred_element_type=F32)
        acc_ref[h * D:(h + 1) * D, :] = alpha * acc_ref[h * D:(h + 1) * D, :] + pv[:D]
        l_ref[h:h + 1, :] = alpha * l_ref[h:h + 1, :] + pv[D:D + 1]
        m_ref[h:h + 1, :] = m_new

    @pl.when(j == (i * tq + tq - 1) // tk)
    def _():
        for h in range(H):
            o_ref[h * D:(h + 1) * D, :] = (acc_ref[h * D:(h + 1) * D, :] / l_ref[h:h + 1, :]).astype(o_ref.dtype)


def attention(ct, k, vt_aug, bias, T, tq, tk):
    H, D, DA = DSA_HEADS, DSA_DIM, DSA_DIM_AUG
    tqb = bias.shape[2]
    pairs = [(i, j) for i in range(T // tq) for j in range((i * tq + tq - 1) // tk + 1)]
    ii = jnp.asarray([p[0] for p in pairs], I32)
    jj = jnp.asarray([p[1] for p in pairs], I32)
    grid_spec = pltpu.PrefetchScalarGridSpec(
        num_scalar_prefetch=2,
        grid=(len(pairs),),
        in_specs=[pl.BlockSpec((H * D, tq), lambda s, ii, jj: (CT_AQ // (H * D), ii[s])),
                  pl.BlockSpec((tk, H * D), lambda s, ii, jj: (jj[s], 0)),
                  pl.BlockSpec((H * DA, tk), lambda s, ii, jj: (0, jj[s])),
                  pl.BlockSpec((tq // tqb, tk, tqb), lambda s, ii, jj: (ii[s], jj[s], 0))],
        out_specs=pl.BlockSpec((H * D, tq), lambda s, ii, jj: (0, ii[s])),
        scratch_shapes=[pltpu.VMEM((H, tq), F32), pltpu.VMEM((H, tq), F32), pltpu.VMEM((H * D, tq), F32)],
    )
    return pl.pallas_call(
        functools.partial(_attn_kernel, tq=tq, tk=tk),
        grid_spec=grid_spec,
        out_shape=jax.ShapeDtypeStruct((H * D, T), BF16),
        compiler_params=_cparams(("arbitrary",)),
        name="attention",
    )(ii, jj, ct, k, vt_aug, bias)


def _idx_sample_kernel(pt_ref, q_ref, w_ref, kn_ref, cache_ref, bias_ref, kbuf_ref, key_ref, sem, *,
                       n_pages, ck, k_sel, t_dec):
    b = pl.program_id(0)
    n_chunks = key_ref.shape[0]

    def page_copy(pg):
        return pltpu.make_async_copy(cache_ref.at[pt_ref[b, pg]],
                                     kbuf_ref.at[:, pl.ds(pl.multiple_of(pg * PAGE, PAGE), PAGE)], sem)

    def start(pg, c):
        page_copy(pg).start()
        return c

    lax.fori_loop(0, n_pages, start, 0)
    kbuf_ref[:, n_pages * PAGE:(n_pages + 1) * PAGE] = kn_ref[0]
    kbuf_ref[:, (n_pages + 1) * PAGE:] = jnp.zeros((IDX_DIM, ck - PAGE), F32)

    def wait(pg, c):
        page_copy(pg).wait()
        return c

    lax.fori_loop(0, n_pages, wait, 0)

    q = q_ref[0]
    wcol = w_ref[0]
    t_pos = n_pages * PAGE + lax.broadcasted_iota(I32, (t_dec, 1), 0)

    def score_chunk(c, carry):
        kc = kbuf_ref[:, pl.ds(pl.multiple_of(c * ck, ck), ck)].astype(BF16)
        d = jnp.dot(q, kc, preferred_element_type=F32)
        d = jnp.maximum(d, 0.0) * wcol
        sc = d[0:t_dec]
        for h in range(1, IDX_HEADS):
            sc = sc + d[h * t_dec:(h + 1) * t_dec]
        s_pos = c * ck + lax.broadcasted_iota(I32, (t_dec, ck), 1)
        key_ref[c] = _sortable_key(jnp.where(s_pos <= t_pos, sc, -jnp.inf))
        return carry

    lax.fori_loop(0, n_chunks, score_chunk, 0)

    ktarget = jnp.full((t_dec, 1), k_sel, I32)
    thr, cut = _select_threshold(key_ref, n_chunks, ktarget, lambda c: c * ck, t_dec, unroll=True)

    def write_sel(c, carry):
        blk = key_ref[c]
        pos = c * ck + lax.broadcasted_iota(I32, (t_dec, ck), 1)
        sel = (blk > thr) | ((blk == thr) & (pos <= cut))
        bias_ref[0, c] = jnp.where(sel, 0.0, MASKED)
        return carry

    lax.fori_loop(0, n_chunks, write_sel, 0)


def idx_select_sample(iq, iw, kn, cache_idx_kt, page_table, B, t_dec, ck):
    n_pages = page_table.shape[1]
    past = n_pages * PAGE
    k_sel = min(TOPK_MAX, (past + t_dec) // 4)
    n_chunks = past // ck + 1
    iw = iw.reshape(B, t_dec, IDX_HEADS)
    iw = (iw * (IDX_HEADS ** -0.5 * IDX_DIM ** -0.5)).transpose(0, 2, 1).reshape(B, IDX_HEADS * t_dec, 1)
    kn = jnp.pad(kn.reshape(B, t_dec, IDX_DIM).transpose(0, 2, 1), ((0, 0), (0, 0), (0, PAGE - t_dec)))
    grid_spec = pltpu.PrefetchScalarGridSpec(
        num_scalar_prefetch=1,
        grid=(B,),
        in_specs=[pl.BlockSpec((1, IDX_HEADS * t_dec, IDX_DIM), lambda b, pt: (b, 0, 0)),
                  pl.BlockSpec((1, IDX_HEADS * t_dec, 1), lambda b, pt: (b, 0, 0)),
                  pl.BlockSpec((1, IDX_DIM, PAGE), lambda b, pt: (b, 0, 0)),
                  pl.BlockSpec(memory_space=pl.ANY)],
        out_specs=pl.BlockSpec((1, n_chunks, t_dec, ck), lambda b, pt: (b, 0, 0, 0)),
        scratch_shapes=[pltpu.VMEM((IDX_DIM, past + ck), F32),
                        pltpu.VMEM((n_chunks, t_dec, ck), I32),
                        pltpu.SemaphoreType.DMA(())],
    )
    return pl.pallas_call(
        functools.partial(_idx_sample_kernel, n_pages=n_pages, ck=ck, k_sel=k_sel, t_dec=t_dec),
        grid_spec=grid_spec,
        out_shape=jax.ShapeDtypeStruct((B, n_chunks, t_dec, ck), F32),
        compiler_params=_cparams(("arbitrary",)),
        name="idx_select_sample",
    )(page_table, iq, iw, kn, cache_idx_kt)


KEY_GROUP = 64


def _attn_sample_kernel(pt_ref, q_ref, kn_ref, vn_ref, b_ref, hm_ref, ex_ref, *refs, pg, t_dec):
    H, D = DSA_HEADS, DSA_DIM
    k_refs, v_refs = refs[:pg], refs[pg:2 * pg]
    o_ref, m_ref, l_ref, acc_ref = refs[2 * pg:]
    j = pl.program_id(1)
    nj = pl.num_programs(1)
    n_groups = pg * PAGE // KEY_GROUP

    @pl.when(j == 0)
    def _():
        m_ref[...] = jnp.full(m_ref.shape, M_INIT, F32)
        l_ref[...] = jnp.zeros(l_ref.shape, F32)
        acc_ref[...] = jnp.zeros(acc_ref.shape, F32)

    def update(kc, vc):
        b = b_ref[0, 0]
        by_group = jnp.concatenate([b[:, g * KEY_GROUP:(g + 1) * KEY_GROUP] for g in range(n_groups)], axis=0)
        wide = jnp.dot(by_group.astype(BF16), ex_ref[...], preferred_element_type=F32)
        hm = hm_ref[...]
        bias = jnp.concatenate(
            [jnp.concatenate([wide[g * t_dec:(g + 1) * t_dec]] * H, axis=0) + hm for g in range(n_groups)], axis=1)
        s = lax.dot_general(q_ref[0], kc, (((1,), (1,)), ((), ())), preferred_element_type=F32)
        s = s * (D ** -0.5) + bias
        m_prev = m_ref[...]
        m_new = jnp.maximum(m_prev, jnp.max(s, axis=-1, keepdims=True))
        alpha = jnp.exp(m_prev - m_new)
        pr = jnp.exp(s - m_new)
        l_ref[...] = alpha * l_ref[...] + jnp.sum(pr, axis=-1, keepdims=True)
        acc_ref[...] = alpha * acc_ref[...] + jnp.dot(pr.astype(BF16), vc, preferred_element_type=F32)
        m_ref[...] = m_new

    @pl.when(j < nj - 1)
    def _():
        kc = jnp.concatenate([r[0] for r in k_refs], axis=0).astype(BF16)
        vc = jnp.concatenate([r[0] for r in v_refs], axis=0).astype(BF16)
        update(kc, vc)

    @pl.when(j == nj - 1)
    def _():
        pad = jnp.zeros(((pg * PAGE - t_dec) * H, D), F32)
        kc = jnp.concatenate([kn_ref[0], pad], axis=0).astype(BF16)
        vc = jnp.concatenate([vn_ref[0], pad], axis=0).astype(BF16)
        update(kc, vc)
        out = acc_ref[...] / l_ref[...]
        for h in range(H):
            o_ref[0, :, h * D:(h + 1) * D] = out[h * t_dec:(h + 1) * t_dec].astype(o_ref.dtype)


def attention_sample(q, kn, vn, cache_k, cache_v, page_table, bias, B, t_dec, pg):
    H, D = DSA_HEADS, DSA_DIM
    n_pages = page_table.shape[1]
    nj = n_pages // pg + 1
    gl = KEY_GROUP * H
    q_head = jnp.arange(H * t_dec, dtype=I32)[:, None] // t_dec
    k_head = jnp.arange(gl, dtype=I32)[None, :] % H
    head_mask = jnp.where(q_head == k_head, 0.0, MASKED).astype(F32)
    expand = (jnp.arange(KEY_GROUP, dtype=I32)[:, None] == jnp.arange(gl, dtype=I32)[None, :] // H).astype(BF16)

    def page_map(u):
        return lambda b, j, pt: (pt[b, jnp.minimum(j, n_pages // pg - 1) * pg + u], 0, 0)

    page_specs = [pl.BlockSpec((1, PAGE * H, D), page_map(u)) for u in range(pg)]
    grid_spec = pltpu.PrefetchScalarGridSpec(
        num_scalar_prefetch=1,
        grid=(B, nj),
        in_specs=[pl.BlockSpec((1, H * t_dec, D), lambda b, j, pt: (b, 0, 0)),
                  pl.BlockSpec((1, t_dec * H, D), lambda b, j, pt: (b, 0, 0)),
                  pl.BlockSpec((1, t_dec * H, D), lambda b, j, pt: (b, 0, 0)),
                  pl.BlockSpec((1, 1, t_dec, pg * PAGE), lambda b, j, pt: (b, j, 0, 0)),
                  pl.BlockSpec((H * t_dec, gl), lambda b, j, pt: (0, 0)),
                  pl.BlockSpec((KEY_GROUP, gl), lambda b, j, pt: (0, 0))]
        + page_specs + page_specs,
        out_specs=pl.BlockSpec((1, t_dec, H * D), lambda b, j, pt: (b, 0, 0)),
        scratch_shapes=[pltpu.VMEM((H * t_dec, 1), F32), pltpu.VMEM((H * t_dec, 1), F32),
                        pltpu.VMEM((H * t_dec, D), F32)],
    )
    return pl.pallas_call(
        functools.partial(_attn_sample_kernel, pg=pg, t_dec=t_dec),
        grid_spec=grid_spec,
        out_shape=jax.ShapeDtypeStruct((B, t_dec, H * D), F32),
        compiler_params=_cparams(("parallel", "arbitrary")),
        name="attention_sample",
    )(page_table, q, kn, vn, bias, head_mask, expand, *([cache_k] * pg), *([cache_v] * pg))


def _out_proj_kernel(x_ref, a_ref, b_ref, wa_ref, wb_ref, o_ref):
    o_ref[...] = (x_ref[...] + jnp.dot(a_ref[...], wa_ref[...], preferred_element_type=F32)
                  + jnp.dot(b_ref[...], wb_ref[...], preferred_element_type=F32))


def out_proj(x, a, b, wa, wb, tm, tn):
    M, D = x.shape
    Ka, Kb = a.shape[1], b.shape[1]
    return pl.pallas_call(
        _out_proj_kernel,
        grid=(M // tm, D // tn),
        in_specs=[pl.BlockSpec((tm, tn), lambda i, j: (i, j)),
                  pl.BlockSpec((tm, Ka), lambda i, j: (i, 0)),
                  pl.BlockSpec((tm, Kb), lambda i, j: (i, 0)),
                  pl.BlockSpec((Ka, tn), lambda i, j: (0, j)),
                  pl.BlockSpec((Kb, tn), lambda i, j: (0, j))],
        out_specs=pl.BlockSpec((tm, tn), lambda i, j: (i, j)),
        out_shape=jax.ShapeDtypeStruct((M, D), F32),
        compiler_params=_cparams(("parallel", "arbitrary")),
        name="out_proj",
    )(x, a, b, wa, wb)


def _ffn_kernel(x_ref, g_ref, wu_ref, wd_ref, gf_ref, o_ref, h_ref, acc_ref):
    f = pl.program_id(1)

    @pl.when(f == 0)
    def _():
        x = x_ref[...]
        ms = jnp.mean(x * x, axis=-1, keepdims=True)
        h_ref[...] = (x * lax.rsqrt(ms + EPS) * g_ref[...]).astype(BF16)
        acc_ref[...] = jnp.zeros(acc_ref.shape, F32)

    u = jnp.dot(h_ref[...], wu_ref[...], preferred_element_type=F32)
    u = jnp.square(jnp.maximum(u, 0.0)).astype(BF16)
    acc_ref[...] += jnp.dot(u, wd_ref[...], preferred_element_type=F32)

    @pl.when(f == pl.num_programs(1) - 1)
    def _():
        x2 = x_ref[...] + acc_ref[...]
        ms = jnp.mean(x2 * x2, axis=-1, keepdims=True)
        o_ref[...] = x2 * lax.rsqrt(ms + EPS) * gf_ref[...]


def ffn(x, g_ffn, w_up, w_down, g_final, tm, tf):
    M, D = x.shape
    Fd = w_up.shape[1]
    return pl.pallas_call(
        _ffn_kernel,
        grid=(M // tm, Fd // tf),
        in_specs=[pl.BlockSpec((tm, D), lambda i, f: (i, 0)),
                  pl.BlockSpec((1, D), lambda i, f: (0, 0)),
                  pl.BlockSpec((D, tf), lambda i, f: (0, f)),
                  pl.BlockSpec((tf, D), lambda i, f: (f, 0)),
                  pl.BlockSpec((1, D), lambda i, f: (0, 0))],
        out_specs=pl.BlockSpec((tm, D), lambda i, f: (i, 0)),
        out_shape=jax.ShapeDtypeStruct((M, D), F32),
        scratch_shapes=[pltpu.VMEM((tm, D), BF16), pltpu.VMEM((tm, D), F32)],
        compiler_params=_cparams(("parallel", "arbitrary")),
        name="ffn",
    )(x, g_ffn.reshape(1, D), w_up, w_down, g_final.reshape(1, D))


def _pack_w_in(w_in):
    D = w_in.shape[0]
    gq, gk, gv, gg, ga, aq, ak, av, iq, iw, ik = jnp.split(
        w_in, [512, 1024, 2048, 3072, 3088, 4112, 5136, 6160, 7184, 7200], axis=1)
    pad = jnp.zeros((D, PA_WIDTH - C_SMALL - 96), w_in.dtype)
    w_a = jnp.concatenate([gq, gk, gv, gg, ik, ga, iw, pad], axis=1).astype(BF16)
    w_b = jnp.concatenate([ak, av], axis=1).astype(BF16)
    w_ct = jnp.concatenate([aq, av, iq], axis=1).T.astype(BF16)
    return w_a, w_b, w_ct


def _tile(n, pref):
    return pref if n % pref == 0 else n


def kernel(x_prompt, x_sample, cache_k, cache_v, cache_idx_k, state_gla, page_table,
           g_mix, w_in, w_gate_up, b_gate, g_gla_out, w_o, g_ffn, w_up, w_down, g_final):
    depth = w_in.shape[0]
    Bp, Tp, D = x_prompt.shape
    Bs, Ts, _ = x_sample.shape
    assert Bp == 1 and depth == 1
    H, Dh = DSA_HEADS, DSA_DIM
    n_pool = cache_k.shape[1]

    xp = x_prompt.reshape(Bp * Tp, D)
    xs = x_sample.reshape(Bs * Ts, D)
    l = 0
    w_a, w_b, w_ct = _pack_w_in(w_in[l])
    wo_a = w_o[l, :GLA_HEADS * GLA_DV].astype(BF16)
    wo_b = w_o[l, GLA_HEADS * GLA_DV:].astype(BF16)
    wu = w_up[l].astype(BF16)
    wd = w_down[l].astype(BF16)
    idx_scale = IDX_HEADS ** -0.5 * IDX_DIM ** -0.5

    tm = _tile(Tp, 512)
    pa_p = norm_matmul(xp, g_mix[l], w_a, tm, PA_WIDTH // 2)
    k3_p, v3_p, kb_p = norm_matmul_heads(xp, g_mix[l], w_b, tm)
    ct_p = norm_matmul_t(xp, g_mix[l], w_ct, tm, 1024)
    s0 = jnp.zeros((Bp, GLA_HEADS, GLA_DK, GLA_DV), F32)
    o_gla_p, s_fin_p = gla(pa_p, w_gate_up[l], b_gate[l], g_gla_out[l], s0, Bp, Tp, _tile(Tp, 128), 16)
    ik_p = pa_p[:, C_SMALL + SM_IK:C_SMALL + SM_IK + IDX_DIM]
    wt = (pa_p[:, C_SMALL + SM_IW:C_SMALL + SM_IW + IDX_HEADS] * idx_scale).T
    bias = idx_select(ct_p, wt, ik_p.astype(BF16), Tp, _tile(Tp, 256), _tile(Tp, 512))
    vt = ct_p[CT_AV:CT_AV + H * Dh].reshape(H, Dh, Tp)
    vt_aug = jnp.concatenate([vt, jnp.ones((H, DSA_DIM_AUG - Dh, Tp), BF16)], axis=1).reshape(H * DSA_DIM_AUG, Tp)
    o_dsa_p = attention(ct_p, kb_p, vt_aug, bias, Tp, tm, _tile(Tp, 1024)).T
    x1p = out_proj(xp, o_gla_p, o_dsa_p, wo_a, wo_b, tm, 1024)
    y_p = ffn(x1p, g_ffn[l], wu, wd, g_final, tm, 1024)

    Ms = Bs * Ts
    pa_s = norm_matmul(xs, g_mix[l], w_a, Ms, PA_WIDTH // 2)
    k3_s, v3_s, _ = norm_matmul_heads(xs, g_mix[l], w_b, Ms)
    ct_s = norm_matmul_t(xs, g_mix[l], w_ct, Ms, 1024)
    o_gla_s, s_fin_s = gla(pa_s, w_gate_up[l], b_gate[l], g_gla_out[l], state_gla[l], Bs, Ts, Ts, Ts)
    ik_s = pa_s[:, C_SMALL + SM_IK:C_SMALL + SM_IK + IDX_DIM]
    iq_s = ct_s[CT_IQ:CT_IQ + IDX_HEADS * IDX_DIM].reshape(IDX_HEADS, IDX_DIM, Bs, Ts)
    iq_s = iq_s.transpose(2, 0, 3, 1).reshape(Bs, IDX_HEADS * Ts, IDX_DIM)
    pg = 8
    bias_s = idx_select_sample(iq_s, pa_s[:, C_SMALL + SM_IW:C_SMALL + SM_IW + IDX_HEADS], ik_s,
                               jnp.swapaxes(cache_idx_k[l], 1, 2), page_table, Bs, Ts, pg * PAGE)
    q_s = ct_s[CT_AQ:CT_AQ + H * Dh].reshape(H, Dh, Bs, Ts).transpose(2, 0, 3, 1).reshape(Bs, H * Ts, Dh)
    o_dsa_s = attention_sample(q_s, k3_s.reshape(Bs, Ts * H, Dh), v3_s.reshape(Bs, Ts * H, Dh),
                               cache_k[l].reshape(n_pool, PAGE * H, Dh), cache_v[l].reshape(n_pool, PAGE * H, Dh),
                               page_table, bias_s, Bs, Ts, pg)
    x1s = out_proj(xs, o_gla_s, o_dsa_s.reshape(Ms, H * Dh).astype(BF16), wo_a, wo_b, Ms, 1024)
    y_s = ffn(x1s, g_ffn[l], wu, wd, g_final, Ms, 512)

    return (y_p.reshape(Bp, Tp, D), y_s.reshape(Bs, Ts, D),
            k3_p.reshape(1, Bp, Tp, H, Dh), v3_p.reshape(1, Bp, Tp, H, Dh),
            ik_p.reshape(1, Bp, Tp, IDX_DIM), s_fin_p[None],
            k3_s.reshape(1, Bs, Ts, H, Dh), v3_s.reshape(1, Bs, Ts, H, Dh),
            ik_s.reshape(1, Bs, Ts, IDX_DIM), s_fin_s[None])
```

```python
import functools
import math

import jax
import jax.numpy as jnp
from jax import lax
from jax.experimental import pallas as pl
from jax.experimental.pallas import tpu as pltpu

F32 = jnp.float32
BF16 = jnp.bfloat16
I32 = jnp.int32

EPS = 1e-6
GATE_TAU = 16.0
GLA_HEADS = 4
GLA_DK = 128
GLA_DV = 256
GATE_RANK = 16
DSA_HEADS = 8
DSA_DIM = 128
IDX_HEADS = 16
IDX_DIM = 64
TOPK_MAX = 256
PAGE = 128

LANES = 128
MASKED = -1e30
M_INIT = -1e29
INT_MIN = -(2 ** 31)
VMEM_LIMIT = 52 * 1024 * 1024
IDX_VMEM_LIMIT = 56 * 1024 * 1024

C_GQ, C_GK, C_GV, C_GG = 0, 512, 1024, 2048
C_SMALL = 3072
SM_IK, SM_GA, SM_IW = 0, 64, 80
PA_WIDTH = 3328
CT_AQ, CT_AV, CT_IQ = 0, 1024, 2048
CT_ROWS = 3072


def _cparams(sem):
    return pltpu.CompilerParams(dimension_semantics=sem, vmem_limit_bytes=VMEM_LIMIT)


def _norm_mm_kernel(x_ref, g_ref, w_ref, o_ref, h_ref):
    @pl.when(pl.program_id(1) == 0)
    def _():
        x = x_ref[...]
        ms = jnp.mean(x * x, axis=-1, keepdims=True)
        h_ref[...] = (x * lax.rsqrt(ms + EPS) * g_ref[...]).astype(BF16)

    o_ref[...] = jnp.dot(h_ref[...], w_ref[...], preferred_element_type=F32)


def norm_matmul(x, g, w, tm, tn):
    M, D = x.shape
    N = w.shape[1]
    return pl.pallas_call(
        _norm_mm_kernel,
        grid=(M // tm, N // tn),
        in_specs=[pl.BlockSpec((tm, D), lambda i, j: (i, 0)),
                  pl.BlockSpec((1, D), lambda i, j: (0, 0)),
                  pl.BlockSpec((D, tn), lambda i, j: (0, j))],
        out_specs=pl.BlockSpec((tm, tn), lambda i, j: (i, j)),
        out_shape=jax.ShapeDtypeStruct((M, N), F32),
        scratch_shapes=[pltpu.VMEM((tm, D), BF16)],
        compiler_params=_cparams(("parallel", "arbitrary")),
        name="norm_matmul",
    )(x, g.reshape(1, D), w)


def _norm_mm_heads_kernel(x_ref, g_ref, w_ref, k3_ref, v3_ref, kb_ref, h_ref):
    H, D = DSA_HEADS, DSA_DIM
    j = pl.program_id(1)

    @pl.when(j == 0)
    def _():
        x = x_ref[...]
        ms = jnp.mean(x * x, axis=-1, keepdims=True)
        h_ref[...] = (x * lax.rsqrt(ms + EPS) * g_ref[...]).astype(BF16)

    r = jnp.dot(h_ref[...], w_ref[...], preferred_element_type=F32)

    @pl.when(j == 0)
    def _():
        kb_ref[...] = r.astype(BF16)
        for h in range(H):
            k3_ref[:, h, :] = r[:, h * D:(h + 1) * D]

    @pl.when(j == 1)
    def _():
        for h in range(H):
            v3_ref[:, h, :] = r[:, h * D:(h + 1) * D]


def norm_matmul_heads(x, g, w, tm):
    M, D = x.shape
    H, Dh = DSA_HEADS, DSA_DIM
    return pl.pallas_call(
        _norm_mm_heads_kernel,
        grid=(M // tm, 2),
        in_specs=[pl.BlockSpec((tm, D), lambda i, j: (i, 0)),
                  pl.BlockSpec((1, D), lambda i, j: (0, 0)),
                  pl.BlockSpec((D, H * Dh), lambda i, j: (0, j))],
        out_specs=[pl.BlockSpec((tm, H, Dh), lambda i, j: (i, 0, 0)),
                   pl.BlockSpec((tm, H, Dh), lambda i, j: (i, 0, 0)),
                   pl.BlockSpec((tm, H * Dh), lambda i, j: (i, 0))],
        out_shape=[jax.ShapeDtypeStruct((M, H, Dh), F32), jax.ShapeDtypeStruct((M, H, Dh), F32),
                   jax.ShapeDtypeStruct((M, H * Dh), BF16)],
        scratch_shapes=[pltpu.VMEM((tm, D), BF16)],
        compiler_params=_cparams(("parallel", "arbitrary")),
        name="norm_matmul_heads",
    )(x, g.reshape(1, D), w)


def _norm_mm_t_kernel(x_ref, g_ref, wt_ref, o_ref, h_ref):
    @pl.when(pl.program_id(1) == 0)
    def _():
        x = x_ref[...]
        ms = jnp.mean(x * x, axis=-1, keepdims=True)
        h_ref[...] = (x * lax.rsqrt(ms + EPS) * g_ref[...]).astype(BF16)

    o_ref[...] = lax.dot_general(wt_ref[...], h_ref[...], (((1,), (1,)), ((), ())),
                                 preferred_element_type=F32).astype(o_ref.dtype)


def norm_matmul_t(x, g, wt, tm, tn):
    M, D = x.shape
    N = wt.shape[0]
    return pl.pallas_call(
        _norm_mm_t_kernel,
        grid=(M // tm, N // tn),
        in_specs=[pl.BlockSpec((tm, D), lambda i, j: (i, 0)),
                  pl.BlockSpec((1, D), lambda i, j: (0, 0)),
                  pl.BlockSpec((tn, D), lambda i, j: (j, 0))],
        out_specs=pl.BlockSpec((tn, tm), lambda i, j: (j, i)),
        out_shape=jax.ShapeDtypeStruct((N, M), BF16),
        scratch_shapes=[pltpu.VMEM((tm, D), BF16)],
        compiler_params=_cparams(("parallel", "arbitrary")),
        name="norm_matmul_t",
    )(x, g.reshape(1, D), wt)


def _gla_kernel(q_ref, k_ref, v_ref, g_ref, sm_ref, wg_ref, bg_ref, go_ref, s0_ref,
                o_ref, sfin_ref, st_ref, b_ref, oacc_ref, *, tb, C):
    H, DK, DV = GLA_HEADS, GLA_DK, GLA_DV
    t = pl.program_id(1)

    @pl.when(t == 0)
    def _():
        for h in range(H):
            st_ref[h] = s0_ref[0, h].T

    a_lr = sm_ref[:, SM_GA:SM_GA + GATE_RANK]
    z = jnp.dot(a_lr.astype(BF16), wg_ref[...], preferred_element_type=F32) + bg_ref[...]
    log_a = (jnp.minimum(z, 0.0) - jnp.log1p(jnp.exp(-jnp.abs(z)))) / GATE_TAU
    row = lax.broadcasted_iota(I32, (tb, H * DK), 0) & (C - 1)
    b = log_a
    sh = 1
    while sh < C:
        b = b + jnp.where(row >= sh, pltpu.roll(b, sh, axis=0), 0.0)
        sh *= 2
    b_ref[...] = b

    jj = lax.broadcasted_iota(I32, (C, 1), 0)

    def chunk(c, carry):
        r0 = pl.multiple_of(c * C, C)
        for h in range(H):
            bq = b_ref[pl.ds(r0, C), h * DK:(h + 1) * DK]
            q = q_ref[pl.ds(r0, C), h * DK:(h + 1) * DK] * (DK ** -0.5)
            k = k_ref[pl.ds(r0, C), h * DK:(h + 1) * DK]
            v = v_ref[pl.ds(r0, C), h * DV:(h + 1) * DV]
            st = st_ref[h]
            qe = q * jnp.exp(bq)
            o = lax.dot_general(qe.astype(BF16), st.astype(BF16), (((1,), (1,)), ((), ())),
                                preferred_element_type=F32)
            rows = []
            for i in range(C):
                n = 8 if (C > 8 and i < 8) else C
                causal = jj[:n] <= i
                d = jnp.where(causal, bq[i:i + 1, :] - bq[:n], 0.0)
                a = jnp.sum(q[i:i + 1, :] * k[:n] * jnp.exp(d), axis=-1, keepdims=True)
                a = jnp.where(causal, a, 0.0)
                rows.append(jnp.sum(a * v[:n], axis=0, keepdims=True))
            o = o + jnp.concatenate(rows, axis=0)
            oacc_ref[pl.ds(r0, C), h * DV:(h + 1) * DV] = o
            b_last = bq[C - 1:C, :]
            ke = k * jnp.exp(b_last - bq)
            kv_t = lax.dot_general(v.astype(BF16), ke.astype(BF16), (((0,), (0,)), ((), ())),
                                   preferred_element_type=F32)
            st_ref[h] = st * jnp.exp(b_last) + kv_t
        return carry

    lax.fori_loop(0, tb // C, chunk, 0)

    gate = g_ref[...]
    gate = gate * (1.0 / (1.0 + jnp.exp(-gate)))
    for h in range(H):
        o = oacc_ref[:, h * DV:(h + 1) * DV]
        ms = jnp.mean(o * o, axis=-1, keepdims=True)
        y = o * lax.rsqrt(ms + EPS) * go_ref[...]
        o_ref[:, h * DV:(h + 1) * DV] = (y * gate[:, h * DV:(h + 1) * DV]).astype(o_ref.dtype)

    @pl.when(t == pl.num_programs(1) - 1)
    def _():
        for h in range(H):
            sfin_ref[0, h] = st_ref[h].T


def gla(p, w_gate_up, b_gate, g_gla_out, s0, B, T, tb, C):
    H, DK, DV = GLA_HEADS, GLA_DK, GLA_DV
    nt = T // tb
    row = lambda b, t: b * nt + t
    return pl.pallas_call(
        functools.partial(_gla_kernel, tb=tb, C=C),
        grid=(B, nt),
        in_specs=[pl.BlockSpec((tb, H * DK), lambda b, t: (row(b, t), C_GQ // (H * DK))),
                  pl.BlockSpec((tb, H * DK), lambda b, t: (row(b, t), C_GK // (H * DK))),
                  pl.BlockSpec((tb, H * DV), lambda b, t: (row(b, t), C_GV // (H * DV))),
                  pl.BlockSpec((tb, H * DV), lambda b, t: (row(b, t), C_GG // (H * DV))),
                  pl.BlockSpec((tb, LANES), lambda b, t: (row(b, t), C_SMALL // LANES)),
                  pl.BlockSpec((GATE_RANK, H * DK), lambda b, t: (0, 0)),
                  pl.BlockSpec((1, H * DK), lambda b, t: (0, 0)),
                  pl.BlockSpec((1, DV), lambda b, t: (0, 0)),
                  pl.BlockSpec((1, H, DK, DV), lambda b, t: (b, 0, 0, 0))],
        out_specs=[pl.BlockSpec((tb, H * DV), lambda b, t: (row(b, t), 0)),
                   pl.BlockSpec((1, H, DK, DV), lambda b, t: (b, 0, 0, 0))],
        out_shape=[jax.ShapeDtypeStruct((B * T, H * DV), BF16),
                   jax.ShapeDtypeStruct((B, H, DK, DV), F32)],
        scratch_shapes=[pltpu.VMEM((H, DV, DK), F32),
                        pltpu.VMEM((tb, H * DK), F32),
                        pltpu.VMEM((tb, H * DV), F32)],
        compiler_params=_cparams(("arbitrary", "arbitrary")),
        name="gla",
    )(p, p, p, p, p, w_gate_up.astype(BF16), b_gate.reshape(1, -1), g_gla_out.reshape(1, -1), s0)


def _sortable_key(score):
    bits = lax.bitcast_convert_type(score + 0.0, I32)
    return bits ^ ((bits >> 31) & 0x7FFFFFFF)


def _select_threshold(key_ref, n_chunks, ktarget, pos0, rows, unroll=False):
    ck = key_ref.shape[-1]

    def count(pred):
        def body(c, cnt):
            blk = key_ref[c]
            pos = pos0(c) + lax.broadcasted_iota(I32, (rows, ck), 1)
            m = jnp.where(pred(blk, pos), 1, 0)
            parts = [m[:, n * LANES:(n + 1) * LANES] for n in range(ck // LANES)]
            while len(parts) > 1:
                parts = [parts[p] + parts[p + 1] for p in range(0, len(parts), 2)]
            return cnt + parts[0]
        cnt = lax.fori_loop(0, n_chunks, body, jnp.zeros((rows, LANES), I32), unroll=unroll)
        return jnp.sum(cnt, axis=-1, keepdims=True)

    def count_ge(thr):
        return count(lambda blk, pos: blk >= thr)

    thr = jnp.where(count_ge(jnp.zeros((rows, 1), I32)) >= ktarget, 0, INT_MIN).astype(I32)

    def bit_step(it, thr):
        cand = thr | jnp.left_shift(jnp.int32(1), 30 - it)
        return jnp.where(count_ge(cand) >= ktarget, cand, thr)

    thr = lax.fori_loop(0, 31, bit_step, thr)
    n_gt = count(lambda blk, pos: blk > thr)
    n_ge = count_ge(thr)
    need = ktarget - n_gt
    extra = jnp.max(n_ge - n_gt - need)

    def tie_cut():
        def step(_, lohi):
            lo, hi = lohi
            mid = lo + ((hi - lo) >> 1)
            c = count(lambda blk, pos: (blk == thr) & (pos <= mid))
            ok = c >= need
            return jnp.where(ok, lo, mid), jnp.where(ok, mid, hi)
        lo = jnp.full((rows, 1), -1, I32)
        hi = jnp.full((rows, 1), 2 ** 30, I32)
        _, hi = lax.fori_loop(0, 31, step, (lo, hi))
        return hi

    cut = lax.cond(extra > 0, tie_cut, lambda: jnp.full((rows, 1), 2 ** 30, I32))
    return thr, cut


ROW_BLOCK = 256
SWEEP_UNROLL = 8


def _bit_transpose32(words):
    a = list(words)
    j, m = 16, 0x0000FFFF
    while j:
        for k in range(32):
            if k & j == 0:
                t = (a[k] ^ lax.shift_right_logical(a[k + j], jnp.int32(j))) & m
                a[k] = a[k] ^ t
                a[k + j] = a[k + j] ^ (t << j)
        j >>= 1
        m = m ^ (m << j)
    return a


def _idx_kernel(iqt_ref, wt_ref, ki_ref, bias_ref, p_ref, c_ref, g_ref, *, tq, ck, k_sel):
    i = pl.program_id(0)
    T = ki_ref.shape[0]
    RB = ROW_BLOCK
    n_all = T // ck
    n_chunks = ((i + 1) * tq + ck - 1) // ck
    n_rb = n_chunks * (ck // RB)
    t_pos = i * tq + lax.broadcasted_iota(I32, (1, tq), 1)
    w = wt_ref[...]

    @pl.when(i == 0)
    def _():
        c_ref[...] = jnp.zeros(c_ref.shape, I32)
        g_ref[...] = jnp.zeros(g_ref.shape, I32)
        p_ref[...] = jnp.zeros(p_ref.shape, I32)

    def score_chunk(c, carry):
        r0 = pl.multiple_of(c * ck, ck)
        kc = ki_ref[pl.ds(r0, ck), :]
        acc = jnp.zeros((ck, tq), F32)
        for h in range(IDX_HEADS):
            d = jnp.dot(kc, iqt_ref[h * IDX_DIM:(h + 1) * IDX_DIM, :], preferred_element_type=F32)
            acc = acc + jnp.maximum(d, 0.0) * w[h:h + 1, :]
        s_pos = r0 + lax.broadcasted_iota(I32, (ck, tq), 0)
        bits = lax.bitcast_convert_type(jnp.where(s_pos <= t_pos, acc, -jnp.inf) + 0.0, I32)
        key = bits ^ ((bits >> 31) | jnp.int32(INT_MIN))
        for half in range(ck // RB):
            rb = c * (ck // RB) + half
            base = half * RB
            words = [key[base + 8 * (31 - x):base + 8 * (31 - x) + 8, :] for x in range(32)]
            planes = _bit_transpose32(words)
            for b in range(32):
                p_ref[b, rb] = planes[31 - b]
            c_ref[rb] = jnp.full((8, tq), -1, I32)
            g_ref[rb] = jnp.zeros((8, tq), I32)
        return carry

    lax.fori_loop(0, n_chunks, score_chunk, 0)

    def lane_total(cnt):
        return jnp.sum(cnt, axis=0, keepdims=True)

    def sweep_row_blocks(per_block):
        unroll = math.gcd(SWEEP_UNROLL, T // RB)

        def body(it, cnt):
            for u in range(unroll):
                cnt = cnt + per_block(it * unroll + u)
            return cnt
        n_it = (n_rb + unroll - 1) // unroll
        return lane_total(lax.fori_loop(0, n_it, body, jnp.zeros((8, tq), I32)))

    def decide(ones_count, k_rem):
        keep_ones = ones_count >= k_rem
        return jnp.where(keep_ones, k_rem, k_rem - ones_count), jnp.where(keep_ones, 0, -1)

    def apply_decision(s, b_prev, flip):
        cs = c_ref[s]
        pp = p_ref[b_prev, s]
        g_ref[s] = g_ref[s] | (cs & pp & flip)
        cs = cs & (pp ^ flip)
        c_ref[s] = cs
        return cs

    top = sweep_row_blocks(lambda s: lax.population_count(c_ref[s] & p_ref[31, s]))
    k_rem, flip = decide(top, jnp.minimum(t_pos + 1, k_sel))

    def sweep(it, state):
        k_rem, flip = state
        b = 30 - it
        ones = sweep_row_blocks(lambda s: lax.population_count(apply_decision(s, b + 1, flip) & p_ref[b, s]))
        return decide(ones, k_rem)

    k_rem, flip = lax.fori_loop(0, 31, sweep, (k_rem, flip))
    n_tie = sweep_row_blocks(lambda s: lax.population_count(apply_decision(s, 0, flip)))
    extra = jnp.max(n_tie - k_rem)

    def write_bias(sel_fn):
        def body(s, carry):
            sel = sel_fn(s)
            for jp in range(16):
                rows = jnp.concatenate([(sel >> (2 * jp)) & 1, (sel >> (2 * jp + 1)) & 1], axis=0)
                r0 = pl.multiple_of(s * RB + 16 * jp, 16)
                bias_ref[0, pl.ds(r0, 16), :] = jnp.where(rows != 0, 0.0, MASKED).astype(bias_ref.dtype)
            return carry
        lax.fori_loop(0, n_rb, body, 0)

    def no_surplus_ties():
        write_bias(lambda s: g_ref[s] | c_ref[s])

    def surplus_ties():
        sub = lax.broadcasted_iota(I32, (8, tq), 0)

        def upto(s, cut):
            nb = jnp.clip(((cut - s * RB - sub) >> 3) + 1, 0, 32)
            return jnp.where(nb >= 32, -1, (jnp.int32(1) << jnp.minimum(nb, 31)) - 1)

        def step(_, lohi):
            lo, hi = lohi
            mid = lo + ((hi - lo) >> 1)
            c = sweep_row_blocks(lambda s: lax.population_count(c_ref[s] & upto(s, mid)))
            ok = c >= k_rem
            return jnp.where(ok, lo, mid), jnp.where(ok, mid, hi)

        _, cut = lax.fori_loop(0, 15, step, (jnp.full((1, tq), -1, I32), jnp.full((1, tq), T - 1, I32)))
        write_bias(lambda s: g_ref[s] | (c_ref[s] & upto(s, cut)))

    lax.cond(extra > 0, surplus_ties, no_surplus_ties)

    def write_rest(c, carry):
        r0 = pl.multiple_of(c * ck, ck)
        bias_ref[0, pl.ds(r0, ck), :] = jnp.full((ck, tq), MASKED, bias_ref.dtype)
        return carry

    lax.fori_loop(n_chunks, n_all, write_rest, 0)


def idx_select(ct, wt, ki, T, tq, ck):
    k_sel = min(TOPK_MAX, T // 4)
    return pl.pallas_call(
        functools.partial(_idx_kernel, tq=tq, ck=ck, k_sel=k_sel),
        grid=(T // tq,),
        in_specs=[pl.BlockSpec((IDX_HEADS * IDX_DIM, tq), lambda i: (CT_IQ // (IDX_HEADS * IDX_DIM), i)),
                  pl.BlockSpec((IDX_HEADS, tq), lambda i: (0, i)),
                  pl.BlockSpec((T, IDX_DIM), lambda i: (0, 0))],
        out_specs=pl.BlockSpec((1, T, tq), lambda i: (i, 0, 0)),
        out_shape=jax.ShapeDtypeStruct((T // tq, T, tq), BF16),
        scratch_shapes=[pltpu.VMEM((32, T // ROW_BLOCK, 8, tq), I32),
                        pltpu.VMEM((T // ROW_BLOCK, 8, tq), I32), pltpu.VMEM((T // ROW_BLOCK, 8, tq), I32)],
        compiler_params=pltpu.CompilerParams(dimension_semantics=("arbitrary",), vmem_limit_bytes=IDX_VMEM_LIMIT),
        name="idx_select",
    )(ct, wt, ki)


DSA_DIM_AUG = DSA_DIM + 16
LOG2E = 1.4426950408889634
QK_AHEAD = 2


def _attn_kernel(ii_ref, jj_ref, qt_ref, k_ref, vt_ref, b_ref, o_ref, m_ref, l_ref, acc_ref, *, tq, tk):
    H, D, DA = DSA_HEADS, DSA_DIM, DSA_DIM_AUG
    s_id = pl.program_id(0)
    i, j = ii_ref[s_id], jj_ref[s_id]

    @pl.when(j == 0)
    def _():
        m_ref[...] = jnp.full(m_ref.shape, M_INIT, F32)
        l_ref[...] = jnp.zeros(l_ref.shape, F32)
        acc_ref[...] = jnp.zeros(acc_ref.shape, F32)

    bias = jnp.concatenate([b_ref[n] for n in range(b_ref.shape[0])], axis=1).astype(F32)

    def qk(h):
        return jnp.dot(k_ref[:, h * D:(h + 1) * D], qt_ref[h * D:(h + 1) * D, :], preferred_element_type=F32)

    pending = [qk(h) for h in range(QK_AHEAD)]
    for h in range(H):
        s = pending.pop(0)
        if h + QK_AHEAD < H:
            pending.append(qk(h + QK_AHEAD))
        s = s * (D ** -0.5 * LOG2E) + bias
        m_prev = m_ref[h:h + 1, :]
        m_new = jnp.maximum(m_prev, jnp.max(s, axis=0, keepdims=True))
        alpha = jnp.exp2(m_prev - m_new)
        pr = jnp.exp2((s - m_new).astype(BF16))
        pv = jnp.dot(vt_ref[h * DA:(h + 1) * DA, :], pr, preferred_element_type=F32)
        acc_ref[h * D:(h + 1) * D, :] = alpha * acc_ref[h * D:(h + 1) * D, :] + pv[:D]
        l_ref[h:h + 1, :] = alpha * l_ref[h:h + 1, :] + pv[D:D + 1]
        m_ref[h:h + 1, :] = m_new

    @pl.when(j == (i * tq + tq - 1) // tk)
    def _():
        for h in range(H):
            o_ref[:, h * D:(h + 1) * D] = (acc_ref[h * D:(h + 1) * D, :] / l_ref[h:h + 1, :]).T.astype(o_ref.dtype)


def attention(ct, k, vt_aug, bias, T, tq, tk):
    H, D, DA = DSA_HEADS, DSA_DIM, DSA_DIM_AUG
    tqb = bias.shape[2]
    pairs = [(i, j) for i in range(T // tq) for j in range((i * tq + tq - 1) // tk + 1)]
    ii = jnp.asarray([p[0] for p in pairs], I32)
    jj = jnp.asarray([p[1] for p in pairs], I32)
    grid_spec = pltpu.PrefetchScalarGridSpec(
        num_scalar_prefetch=2,
        grid=(len(pairs),),
        in_specs=[pl.BlockSpec((H * D, tq), lambda s, ii, jj: (CT_AQ // (H * D), ii[s])),
                  pl.BlockSpec((tk, H * D), lambda s, ii, jj: (jj[s], 0)),
                  pl.BlockSpec((H * DA, tk), lambda s, ii, jj: (0, jj[s])),
                  pl.BlockSpec((tq // tqb, tk, tqb), lambda s, ii, jj: (ii[s], jj[s], 0))],
        out_specs=pl.BlockSpec((tq, H * D), lambda s, ii, jj: (ii[s], 0)),
        scratch_shapes=[pltpu.VMEM((H, tq), F32), pltpu.VMEM((H, tq), F32), pltpu.VMEM((H * D, tq), F32)],
    )
    return pl.pallas_call(
        functools.partial(_attn_kernel, tq=tq, tk=tk),
        grid_spec=grid_spec,
        out_shape=jax.ShapeDtypeStruct((T, H * D), BF16),
        compiler_params=_cparams(("arbitrary",)),
        name="attention",
    )(ii, jj, ct, k, vt_aug, bias)


def _idx_sample_kernel(pt_ref, q_ref, w_ref, kn_ref, cache_ref, bias_ref, kbuf_ref, key_ref, sem, *,
                       n_pages, ck, k_sel, t_dec):
    b = pl.program_id(0)
    n_chunks = key_ref.shape[0]

    def page_copy(pg):
        return pltpu.make_async_copy(cache_ref.at[pt_ref[b, pg]],
                                     kbuf_ref.at[:, pl.ds(pl.multiple_of(pg * PAGE, PAGE), PAGE)], sem)

    def start(pg, c):
        page_copy(pg).start()
        return c

    lax.fori_loop(0, n_pages, start, 0)
    kbuf_ref[:, n_pages * PAGE:(n_pages + 1) * PAGE] = kn_ref[0]
    kbuf_ref[:, (n_pages + 1) * PAGE:] = jnp.zeros((IDX_DIM, ck - PAGE), F32)

    def wait(pg, c):
        page_copy(pg).wait()
        return c

    lax.fori_loop(0, n_pages, wait, 0)

    q = q_ref[0]
    wcol = w_ref[0]
    t_pos = n_pages * PAGE + lax.broadcasted_iota(I32, (t_dec, 1), 0)

    def score_chunk(c, carry):
        kc = kbuf_ref[:, pl.ds(pl.multiple_of(c * ck, ck), ck)].astype(BF16)
        d = jnp.dot(q, kc, preferred_element_type=F32)
        d = jnp.maximum(d, 0.0) * wcol
        sc = d[0:t_dec]
        for h in range(1, IDX_HEADS):
            sc = sc + d[h * t_dec:(h + 1) * t_dec]
        s_pos = c * ck + lax.broadcasted_iota(I32, (t_dec, ck), 1)
        key_ref[c] = _sortable_key(jnp.where(s_pos <= t_pos, sc, -jnp.inf))
        return carry

    lax.fori_loop(0, n_chunks, score_chunk, 0)

    ktarget = jnp.full((t_dec, 1), k_sel, I32)
    thr, cut = _select_threshold(key_ref, n_chunks, ktarget, lambda c: c * ck, t_dec, unroll=True)

    def write_sel(c, carry):
        blk = key_ref[c]
        pos = c * ck + lax.broadcasted_iota(I32, (t_dec, ck), 1)
        sel = (blk > thr) | ((blk == thr) & (pos <= cut))
        bias_ref[0, c] = jnp.where(sel, 0.0, MASKED)
        return carry

    lax.fori_loop(0, n_chunks, write_sel, 0)


def idx_select_sample(iq, iw, kn, cache_idx_kt, page_table, B, t_dec, ck):
    n_pages = page_table.shape[1]
    past = n_pages * PAGE
    k_sel = min(TOPK_MAX, (past + t_dec) // 4)
    n_chunks = past // ck + 1
    iw = iw.reshape(B, t_dec, IDX_HEADS)
    iw = (iw * (IDX_HEADS ** -0.5 * IDX_DIM ** -0.5)).transpose(0, 2, 1).reshape(B, IDX_HEADS * t_dec, 1)
    kn = jnp.pad(kn.reshape(B, t_dec, IDX_DIM).transpose(0, 2, 1), ((0, 0), (0, 0), (0, PAGE - t_dec)))
    grid_spec = pltpu.PrefetchScalarGridSpec(
        num_scalar_prefetch=1,
        grid=(B,),
        in_specs=[pl.BlockSpec((1, IDX_HEADS * t_dec, IDX_DIM), lambda b, pt: (b, 0, 0)),
                  pl.BlockSpec((1, IDX_HEADS * t_dec, 1), lambda b, pt: (b, 0, 0)),
                  pl.BlockSpec((1, IDX_DIM, PAGE), lambda b, pt: (b, 0, 0)),
                  pl.BlockSpec(memory_space=pl.ANY)],
        out_specs=pl.BlockSpec((1, n_chunks, t_dec, ck), lambda b, pt: (b, 0, 0, 0)),
        scratch_shapes=[pltpu.VMEM((IDX_DIM, past + ck), F32),
                        pltpu.VMEM((n_chunks, t_dec, ck), I32),
                        pltpu.SemaphoreType.DMA(())],
    )
    return pl.pallas_call(
        functools.partial(_idx_sample_kernel, n_pages=n_pages, ck=ck, k_sel=k_sel, t_dec=t_dec),
        grid_spec=grid_spec,
        out_shape=jax.ShapeDtypeStruct((B, n_chunks, t_dec, ck), F32),
        compiler_params=_cparams(("arbitrary",)),
        name="idx_select_sample",
    )(page_table, iq, iw, kn, cache_idx_kt)


KEY_GROUP = 64


def _attn_sample_kernel(pt_ref, q_ref, kn_ref, vn_ref, b_ref, hm_ref, ex_ref, *refs, pg, t_dec):
    H, D = DSA_HEADS, DSA_DIM
    k_refs, v_refs = refs[:pg], refs[pg:2 * pg]
    o_ref, m_ref, l_ref, acc_ref = refs[2 * pg:]
    j = pl.program_id(1)
    nj = pl.num_programs(1)
    n_groups = pg * PAGE // KEY_GROUP

    @pl.when(j == 0)
    def _():
        m_ref[...] = jnp.full(m_ref.shape, M_INIT, F32)
        l_ref[...] = jnp.zeros(l_ref.shape, F32)
        acc_ref[...] = jnp.zeros(acc_ref.shape, F32)

    def update(kc, vc):
        b = b_ref[0, 0]
        by_group = jnp.concatenate([b[:, g * KEY_GROUP:(g + 1) * KEY_GROUP] for g in range(n_groups)], axis=0)
        wide = jnp.dot(by_group.astype(BF16), ex_ref[...], preferred_element_type=F32)
        hm = hm_ref[...]
        bias = jnp.concatenate(
            [jnp.concatenate([wide[g * t_dec:(g + 1) * t_dec]] * H, axis=0) + hm for g in range(n_groups)], axis=1)
        s = lax.dot_general(q_ref[0], kc, (((1,), (1,)), ((), ())), preferred_element_type=F32)
        s = s * (D ** -0.5) + bias
        m_prev = m_ref[...]
        m_new = jnp.maximum(m_prev, jnp.max(s, axis=-1, keepdims=True))
        alpha = jnp.exp(m_prev - m_new)
        pr = jnp.exp(s - m_new)
        l_ref[...] = alpha * l_ref[...] + jnp.sum(pr, axis=-1, keepdims=True)
        acc_ref[...] = alpha * acc_ref[...] + jnp.dot(pr.astype(BF16), vc, preferred_element_type=F32)
        m_ref[...] = m_new

    @pl.when(j < nj - 1)
    def _():
        kc = jnp.concatenate([r[0] for r in k_refs], axis=0).astype(BF16)
        vc = jnp.concatenate([r[0] for r in v_refs], axis=0).astype(BF16)
        update(kc, vc)

    @pl.when(j == nj - 1)
    def _():
        pad = jnp.zeros(((pg * PAGE - t_dec) * H, D), F32)
        kc = jnp.concatenate([kn_ref[0], pad], axis=0).astype(BF16)
        vc = jnp.concatenate([vn_ref[0], pad], axis=0).astype(BF16)
        update(kc, vc)
        out = acc_ref[...] / l_ref[...]
        for h in range(H):
            o_ref[0, :, h * D:(h + 1) * D] = out[h * t_dec:(h + 1) * t_dec].astype(o_ref.dtype)


def attention_sample(q, kn, vn, cache_k, cache_v, page_table, bias, B, t_dec, pg):
    H, D = DSA_HEADS, DSA_DIM
    n_pages = page_table.shape[1]
    nj = n_pages // pg + 1
    gl = KEY_GROUP * H
    q_head = jnp.arange(H * t_dec, dtype=I32)[:, None] // t_dec
    k_head = jnp.arange(gl, dtype=I32)[None, :] % H
    head_mask = jnp.where(q_head == k_head, 0.0, MASKED).astype(F32)
    expand = (jnp.arange(KEY_GROUP, dtype=I32)[:, None] == jnp.arange(gl, dtype=I32)[None, :] // H).astype(BF16)

    def page_map(u):
        return lambda b, j, pt: (pt[b, jnp.minimum(j, n_pages // pg - 1) * pg + u], 0, 0)

    page_specs = [pl.BlockSpec((1, PAGE * H, D), page_map(u)) for u in range(pg)]
    grid_spec = pltpu.PrefetchScalarGridSpec(
        num_scalar_prefetch=1,
        grid=(B, nj),
        in_specs=[pl.BlockSpec((1, H * t_dec, D), lambda b, j, pt: (b, 0, 0)),
                  pl.BlockSpec((1, t_dec * H, D), lambda b, j, pt: (b, 0, 0)),
                  pl.BlockSpec((1, t_dec * H, D), lambda b, j, pt: (b, 0, 0)),
                  pl.BlockSpec((1, 1, t_dec, pg * PAGE), lambda b, j, pt: (b, j, 0, 0)),
                  pl.BlockSpec((H * t_dec, gl), lambda b, j, pt: (0, 0)),
                  pl.BlockSpec((KEY_GROUP, gl), lambda b, j, pt: (0, 0))]
        + page_specs + page_specs,
        out_specs=pl.BlockSpec((1, t_dec, H * D), lambda b, j, pt: (b, 0, 0)),
        scratch_shapes=[pltpu.VMEM((H * t_dec, 1), F32), pltpu.VMEM((H * t_dec, 1), F32),
                        pltpu.VMEM((H * t_dec, D), F32)],
    )
    return pl.pallas_call(
        functools.partial(_attn_sample_kernel, pg=pg, t_dec=t_dec),
        grid_spec=grid_spec,
        out_shape=jax.ShapeDtypeStruct((B, t_dec, H * D), F32),
        compiler_params=_cparams(("parallel", "arbitrary")),
        name="attention_sample",
    )(page_table, q, kn, vn, bias, head_mask, expand, *([cache_k] * pg), *([cache_v] * pg))


def _out_proj_kernel(x_ref, a_ref, b_ref, wa_ref, wb_ref, o_ref):
    o_ref[...] = (x_ref[...] + jnp.dot(a_ref[...], wa_ref[...], preferred_element_type=F32)
                  + jnp.dot(b_ref[...], wb_ref[...], preferred_element_type=F32))


def out_proj(x, a, b, wa, wb, tm, tn):
    M, D = x.shape
    Ka, Kb = a.shape[1], b.shape[1]
    return pl.pallas_call(
        _out_proj_kernel,
        grid=(M // tm, D // tn),
        in_specs=[pl.BlockSpec((tm, tn), lambda i, j: (i, j)),
                  pl.BlockSpec((tm, Ka), lambda i, j: (i, 0)),
                  pl.BlockSpec((tm, Kb), lambda i, j: (i, 0)),
                  pl.BlockSpec((Ka, tn), lambda i, j: (0, j)),
                  pl.BlockSpec((Kb, tn), lambda i, j: (0, j))],
        out_specs=pl.BlockSpec((tm, tn), lambda i, j: (i, j)),
        out_shape=jax.ShapeDtypeStruct((M, D), F32),
        compiler_params=_cparams(("parallel", "arbitrary")),
        name="out_proj",
    )(x, a, b, wa, wb)


def _ffn_kernel(x_ref, g_ref, wu_ref, wd_ref, gf_ref, o_ref, h_ref, acc_ref):
    f = pl.program_id(1)

    @pl.when(f == 0)
    def _():
        x = x_ref[...]
        ms = jnp.mean(x * x, axis=-1, keepdims=True)
        h_ref[...] = (x * lax.rsqrt(ms + EPS) * g_ref[...]).astype(BF16)
        acc_ref[...] = jnp.zeros(acc_ref.shape, F32)

    u = jnp.dot(h_ref[...], wu_ref[...], preferred_element_type=F32)
    u = jnp.square(jnp.maximum(u, 0.0)).astype(BF16)
    acc_ref[...] += jnp.dot(u, wd_ref[...], preferred_element_type=F32)

    @pl.when(f == pl.num_programs(1) - 1)
    def _():
        x2 = x_ref[...] + acc_ref[...]
        ms = jnp.mean(x2 * x2, axis=-1, keepdims=True)
        o_ref[...] = x2 * lax.rsqrt(ms + EPS) * gf_ref[...]


def ffn(x, g_ffn, w_up, w_down, g_final, tm, tf):
    M, D = x.shape
    Fd = w_up.shape[1]
    return pl.pallas_call(
        _ffn_kernel,
        grid=(M // tm, Fd // tf),
        in_specs=[pl.BlockSpec((tm, D), lambda i, f: (i, 0)),
                  pl.BlockSpec((1, D), lambda i, f: (0, 0)),
                  pl.BlockSpec((D, tf), lambda i, f: (0, f)),
                  pl.BlockSpec((tf, D), lambda i, f: (f, 0)),
                  pl.BlockSpec((1, D), lambda i, f: (0, 0))],
        out_specs=pl.BlockSpec((tm, D), lambda i, f: (i, 0)),
        out_shape=jax.ShapeDtypeStruct((M, D), F32),
        scratch_shapes=[pltpu.VMEM((tm, D), BF16), pltpu.VMEM((tm, D), F32)],
        compiler_params=_cparams(("parallel", "arbitrary")),
        name="ffn",
    )(x, g_ffn.reshape(1, D), w_up, w_down, g_final.reshape(1, D))


def _pack_w_in(w_in):
    D = w_in.shape[0]
    gq, gk, gv, gg, ga, aq, ak, av, iq, iw, ik = jnp.split(
        w_in, [512, 1024, 2048, 3072, 3088, 4112, 5136, 6160, 7184, 7200], axis=1)
    pad = jnp.zeros((D, PA_WIDTH - C_SMALL - 96), w_in.dtype)
    w_a = jnp.concatenate([gq, gk, gv, gg, ik, ga, iw, pad], axis=1).astype(BF16)
    w_b = jnp.concatenate([ak, av], axis=1).astype(BF16)
    w_ct = jnp.concatenate([aq, av, iq], axis=1).T.astype(BF16)
    return w_a, w_b, w_ct


def _tile(n, pref):
    return pref if n % pref == 0 else n


def kernel(x_prompt, x_sample, cache_k, cache_v, cache_idx_k, state_gla, page_table,
           g_mix, w_in, w_gate_up, b_gate, g_gla_out, w_o, g_ffn, w_up, w_down, g_final):
    depth = w_in.shape[0]
    Bp, Tp, D = x_prompt.shape
    Bs, Ts, _ = x_sample.shape
    assert Bp == 1 and depth == 1
    H, Dh = DSA_HEADS, DSA_DIM
    n_pool = cache_k.shape[1]

    xp = x_prompt.reshape(Bp * Tp, D)
    xs = x_sample.reshape(Bs * Ts, D)
    l = 0
    w_a, w_b, w_ct = _pack_w_in(w_in[l])
    wo_a = w_o[l, :GLA_HEADS * GLA_DV].astype(BF16)
    wo_b = w_o[l, GLA_HEADS * GLA_DV:].astype(BF16)
    wu = w_up[l].astype(BF16)
    wd = w_down[l].astype(BF16)
    idx_scale = IDX_HEADS ** -0.5 * IDX_DIM ** -0.5

    tm = _tile(Tp, 512)
    pa_p = norm_matmul(xp, g_mix[l], w_a, tm, PA_WIDTH // 2)
    k3_p, v3_p, kb_p = norm_matmul_heads(xp, g_mix[l], w_b, tm)
    ct_p = norm_matmul_t(xp, g_mix[l], w_ct, tm, 1024)
    s0 = jnp.zeros((Bp, GLA_HEADS, GLA_DK, GLA_DV), F32)
    o_gla_p, s_fin_p = gla(pa_p, w_gate_up[l], b_gate[l], g_gla_out[l], s0, Bp, Tp, _tile(Tp, 128), 16)
    ik_p = pa_p[:, C_SMALL + SM_IK:C_SMALL + SM_IK + IDX_DIM]
    wt = (pa_p[:, C_SMALL + SM_IW:C_SMALL + SM_IW + IDX_HEADS] * idx_scale).T
    bias = idx_select(ct_p, wt, ik_p.astype(BF16), Tp, _tile(Tp, 256), _tile(Tp, 512))
    vt = ct_p[CT_AV:CT_AV + H * Dh].reshape(H, Dh, Tp)
    vt_aug = jnp.concatenate([vt, jnp.ones((H, DSA_DIM_AUG - Dh, Tp), BF16)], axis=1).reshape(H * DSA_DIM_AUG, Tp)
    o_dsa_p = attention(ct_p, kb_p, vt_aug, bias, Tp, tm, _tile(Tp, 1024))
    x1p = out_proj(xp, o_gla_p, o_dsa_p, wo_a, wo_b, tm, 1024)
    y_p = ffn(x1p, g_ffn[l], wu, wd, g_final, tm, 1024)

    Ms = Bs * Ts
    pa_s = norm_matmul(xs, g_mix[l], w_a, Ms, PA_WIDTH // 2)
    k3_s, v3_s, _ = norm_matmul_heads(xs, g_mix[l], w_b, Ms)
    ct_s = norm_matmul_t(xs, g_mix[l], w_ct, Ms, 1024)
    o_gla_s, s_fin_s = gla(pa_s, w_gate_up[l], b_gate[l], g_gla_out[l], state_gla[l], Bs, Ts, Ts, Ts)
    ik_s = pa_s[:, C_SMALL + SM_IK:C_SMALL + SM_IK + IDX_DIM]
    iq_s = ct_s[CT_IQ:CT_IQ + IDX_HEADS * IDX_DIM].reshape(IDX_HEADS, IDX_DIM, Bs, Ts)
    iq_s = iq_s.transpose(2, 0, 3, 1).reshape(Bs, IDX_HEADS * Ts, IDX_DIM)
    pg = 8
    bias_s = idx_select_sample(iq_s, pa_s[:, C_SMALL + SM_IW:C_SMALL + SM_IW + IDX_HEADS], ik_s,
                               jnp.swapaxes(cache_idx_k[l], 1, 2), page_table, Bs, Ts, pg * PAGE)
    q_s = ct_s[CT_AQ:CT_AQ + H * Dh].reshape(H, Dh, Bs, Ts).transpose(2, 0, 3, 1).reshape(Bs, H * Ts, Dh)
    o_dsa_s = attention_sample(q_s, k3_s.reshape(Bs, Ts * H, Dh), v3_s.reshape(Bs, Ts * H, Dh),
                               cache_k[l].reshape(n_pool, PAGE * H, Dh), cache_v[l].reshape(n_pool, PAGE * H, Dh),
                               page_table, bias_s, Bs, Ts, pg)
    x1s = out_proj(xs, o_gla_s, o_dsa_s.reshape(Ms, H * Dh).astype(BF16), wo_a, wo_b, Ms, 1024)
    y_s = ffn(x1s, g_ffn[l], wu, wd, g_final, Ms, 512)

    return (y_p.reshape(Bp, Tp, D), y_s.reshape(Bs, Ts, D),
            k3_p.reshape(1, Bp, Tp, H, Dh), v3_p.reshape(1, Bp, Tp, H, Dh),
            ik_p.reshape(1, Bp, Tp, IDX_DIM), s_fin_p[None],
            k3_s.reshape(1, Bs, Ts, H, Dh), v3_s.reshape(1, Bs, Ts, H, Dh),
            ik_s.reshape(1, Bs, Ts, IDX_DIM), s_fin_s[None])
```

```python
import functools
import math

import jax
import jax.numpy as jnp
from jax import lax
from jax.experimental import pallas as pl
from jax.experimental.pallas import tpu as pltpu

F32 = jnp.float32
BF16 = jnp.bfloat16
I32 = jnp.int32

EPS = 1e-6
GATE_TAU = 16.0
GLA_HEADS = 4
GLA_DK = 128
GLA_DV = 256
GATE_RANK = 16
DSA_HEADS = 8
DSA_DIM = 128
IDX_HEADS = 16
IDX_DIM = 64
TOPK_MAX = 256
PAGE = 128

LANES = 128
MASKED = -1e30
M_INIT = -1e29
INT_MIN = -(2 ** 31)
VMEM_LIMIT = 52 * 1024 * 1024
IDX_VMEM_LIMIT = 56 * 1024 * 1024

C_GQ, C_GK, C_GV, C_GG = 0, 512, 1024, 2048
C_SMALL = 3072
SM_IK, SM_GA, SM_IW = 0, 64, 80
PA_WIDTH = 3328
CT_AQ, CT_AV, CT_IQ = 0, 1024, 2048
CT_ROWS = 3072


def _cparams(sem):
    return pltpu.CompilerParams(dimension_semantics=sem, vmem_limit_bytes=VMEM_LIMIT)


def _norm_mm_kernel(x_ref, g_ref, w_ref, o_ref, h_ref):
    @pl.when(pl.program_id(1) == 0)
    def _():
        x = x_ref[...]
        ms = jnp.mean(x * x, axis=-1, keepdims=True)
        h_ref[...] = (x * lax.rsqrt(ms + EPS) * g_ref[...]).astype(BF16)

    o_ref[...] = jnp.dot(h_ref[...], w_ref[...], preferred_element_type=F32)


def norm_matmul(x, g, w, tm, tn):
    M, D = x.shape
    N = w.shape[1]
    return pl.pallas_call(
        _norm_mm_kernel,
        grid=(M // tm, N // tn),
        in_specs=[pl.BlockSpec((tm, D), lambda i, j: (i, 0)),
                  pl.BlockSpec((1, D), lambda i, j: (0, 0)),
                  pl.BlockSpec((D, tn), lambda i, j: (0, j))],
        out_specs=pl.BlockSpec((tm, tn), lambda i, j: (i, j)),
        out_shape=jax.ShapeDtypeStruct((M, N), F32),
        scratch_shapes=[pltpu.VMEM((tm, D), BF16)],
        compiler_params=_cparams(("parallel", "arbitrary")),
        name="norm_matmul",
    )(x, g.reshape(1, D), w)


def _norm_mm_heads_kernel(x_ref, g_ref, w_ref, k3_ref, v3_ref, kb_ref, h_ref):
    H, D = DSA_HEADS, DSA_DIM
    j = pl.program_id(1)

    @pl.when(j == 0)
    def _():
        x = x_ref[...]
        ms = jnp.mean(x * x, axis=-1, keepdims=True)
        h_ref[...] = (x * lax.rsqrt(ms + EPS) * g_ref[...]).astype(BF16)

    r = jnp.dot(h_ref[...], w_ref[...], preferred_element_type=F32)

    @pl.when(j == 0)
    def _():
        kb_ref[...] = r.astype(BF16)
        for h in range(H):
            k3_ref[:, h, :] = r[:, h * D:(h + 1) * D]

    @pl.when(j == 1)
    def _():
        for h in range(H):
            v3_ref[:, h, :] = r[:, h * D:(h + 1) * D]


def norm_matmul_heads(x, g, w, tm):
    M, D = x.shape
    H, Dh = DSA_HEADS, DSA_DIM
    return pl.pallas_call(
        _norm_mm_heads_kernel,
        grid=(M // tm, 2),
        in_specs=[pl.BlockSpec((tm, D), lambda i, j: (i, 0)),
                  pl.BlockSpec((1, D), lambda i, j: (0, 0)),
                  pl.BlockSpec((D, H * Dh), lambda i, j: (0, j))],
        out_specs=[pl.BlockSpec((tm, H, Dh), lambda i, j: (i, 0, 0)),
                   pl.BlockSpec((tm, H, Dh), lambda i, j: (i, 0, 0)),
                   pl.BlockSpec((tm, H * Dh), lambda i, j: (i, 0))],
        out_shape=[jax.ShapeDtypeStruct((M, H, Dh), F32), jax.ShapeDtypeStruct((M, H, Dh), F32),
                   jax.ShapeDtypeStruct((M, H * Dh), BF16)],
        scratch_shapes=[pltpu.VMEM((tm, D), BF16)],
        compiler_params=_cparams(("parallel", "arbitrary")),
        name="norm_matmul_heads",
    )(x, g.reshape(1, D), w)


def _norm_mm_t_kernel(x_ref, g_ref, wt_ref, o_ref, h_ref):
    @pl.when(pl.program_id(1) == 0)
    def _():
        x = x_ref[...]
        ms = jnp.mean(x * x, axis=-1, keepdims=True)
        h_ref[...] = (x * lax.rsqrt(ms + EPS) * g_ref[...]).astype(BF16)

    o_ref[...] = lax.dot_general(wt_ref[...], h_ref[...], (((1,), (1,)), ((), ())),
                                 preferred_element_type=F32).astype(o_ref.dtype)


def norm_matmul_t(x, g, wt, tm, tn):
    M, D = x.shape
    N = wt.shape[0]
    return pl.pallas_call(
        _norm_mm_t_kernel,
        grid=(M // tm, N // tn),
        in_specs=[pl.BlockSpec((tm, D), lambda i, j: (i, 0)),
                  pl.BlockSpec((1, D), lambda i, j: (0, 0)),
                  pl.BlockSpec((tn, D), lambda i, j: (j, 0))],
        out_specs=pl.BlockSpec((tn, tm), lambda i, j: (j, i)),
        out_shape=jax.ShapeDtypeStruct((N, M), BF16),
        scratch_shapes=[pltpu.VMEM((tm, D), BF16)],
        compiler_params=_cparams(("parallel", "arbitrary")),
        name="norm_matmul_t",
    )(x, g.reshape(1, D), wt)


def _gla_kernel(q_ref, k_ref, v_ref, g_ref, sm_ref, wg_ref, bg_ref, go_ref, s0_ref,
                o_ref, sfin_ref, st_ref, b_ref, oacc_ref, *, tb, C):
    H, DK, DV = GLA_HEADS, GLA_DK, GLA_DV
    t = pl.program_id(1)

    @pl.when(t == 0)
    def _():
        for h in range(H):
            st_ref[h] = s0_ref[0, h].T

    a_lr = sm_ref[:, SM_GA:SM_GA + GATE_RANK]
    z = jnp.dot(a_lr.astype(BF16), wg_ref[...], preferred_element_type=F32) + bg_ref[...]
    log_a = (jnp.minimum(z, 0.0) - jnp.log1p(jnp.exp(-jnp.abs(z)))) / GATE_TAU
    row = lax.broadcasted_iota(I32, (tb, H * DK), 0) & (C - 1)
    b = log_a
    sh = 1
    while sh < C:
        b = b + jnp.where(row >= sh, pltpu.roll(b, sh, axis=0), 0.0)
        sh *= 2
    b_ref[...] = b

    jj = lax.broadcasted_iota(I32, (C, 1), 0)

    def chunk(c, carry):
        r0 = pl.multiple_of(c * C, C)
        for h in range(H):
            bq = b_ref[pl.ds(r0, C), h * DK:(h + 1) * DK]
            q = q_ref[pl.ds(r0, C), h * DK:(h + 1) * DK] * (DK ** -0.5)
            k = k_ref[pl.ds(r0, C), h * DK:(h + 1) * DK]
            v = v_ref[pl.ds(r0, C), h * DV:(h + 1) * DV]
            st = st_ref[h]
            qe = q * jnp.exp(bq)
            o = lax.dot_general(qe.astype(BF16), st.astype(BF16), (((1,), (1,)), ((), ())),
                                preferred_element_type=F32)
            rows = []
            for i in range(C):
                n = 8 if (C > 8 and i < 8) else C
                causal = jj[:n] <= i
                d = jnp.where(causal, bq[i:i + 1, :] - bq[:n], 0.0)
                a = jnp.sum(q[i:i + 1, :] * k[:n] * jnp.exp(d), axis=-1, keepdims=True)
                a = jnp.where(causal, a, 0.0)
                rows.append(jnp.sum(a * v[:n], axis=0, keepdims=True))
            o = o + jnp.concatenate(rows, axis=0)
            oacc_ref[pl.ds(r0, C), h * DV:(h + 1) * DV] = o
            b_last = bq[C - 1:C, :]
            ke = k * jnp.exp(b_last - bq)
            kv_t = lax.dot_general(v.astype(BF16), ke.astype(BF16), (((0,), (0,)), ((), ())),
                                   preferred_element_type=F32)
            st_ref[h] = st * jnp.exp(b_last) + kv_t
        return carry

    lax.fori_loop(0, tb // C, chunk, 0)

    gate = g_ref[...]
    gate = gate * (1.0 / (1.0 + jnp.exp(-gate)))
    for h in range(H):
        o = oacc_ref[:, h * DV:(h + 1) * DV]
        ms = jnp.mean(o * o, axis=-1, keepdims=True)
        y = o * lax.rsqrt(ms + EPS) * go_ref[...]
        o_ref[:, h * DV:(h + 1) * DV] = (y * gate[:, h * DV:(h + 1) * DV]).astype(o_ref.dtype)

    @pl.when(t == pl.num_programs(1) - 1)
    def _():
        for h in range(H):
            sfin_ref[0, h] = st_ref[h].T


def gla(p, w_gate_up, b_gate, g_gla_out, s0, B, T, tb, C):
    H, DK, DV = GLA_HEADS, GLA_DK, GLA_DV
    nt = T // tb
    row = lambda b, t: b * nt + t
    return pl.pallas_call(
        functools.partial(_gla_kernel, tb=tb, C=C),
        grid=(B, nt),
        in_specs=[pl.BlockSpec((tb, H * DK), lambda b, t: (row(b, t), C_GQ // (H * DK))),
                  pl.BlockSpec((tb, H * DK), lambda b, t: (row(b, t), C_GK // (H * DK))),
                  pl.BlockSpec((tb, H * DV), lambda b, t: (row(b, t), C_GV // (H * DV))),
                  pl.BlockSpec((tb, H * DV), lambda b, t: (row(b, t), C_GG // (H * DV))),
                  pl.BlockSpec((tb, LANES), lambda b, t: (row(b, t), C_SMALL // LANES)),
                  pl.BlockSpec((GATE_RANK, H * DK), lambda b, t: (0, 0)),
                  pl.BlockSpec((1, H * DK), lambda b, t: (0, 0)),
                  pl.BlockSpec((1, DV), lambda b, t: (0, 0)),
                  pl.BlockSpec((1, H, DK, DV), lambda b, t: (b, 0, 0, 0))],
        out_specs=[pl.BlockSpec((tb, H * DV), lambda b, t: (row(b, t), 0)),
                   pl.BlockSpec((1, H, DK, DV), lambda b, t: (b, 0, 0, 0))],
        out_shape=[jax.ShapeDtypeStruct((B * T, H * DV), BF16),
                   jax.ShapeDtypeStruct((B, H, DK, DV), F32)],
        scratch_shapes=[pltpu.VMEM((H, DV, DK), F32),
                        pltpu.VMEM((tb, H * DK), F32),
                        pltpu.VMEM((tb, H * DV), F32)],
        compiler_params=_cparams(("arbitrary", "arbitrary")),
        name="gla",
    )(p, p, p, p, p, w_gate_up.astype(BF16), b_gate.reshape(1, -1), g_gla_out.reshape(1, -1), s0)


def _sortable_key(score):
    bits = lax.bitcast_convert_type(score + 0.0, I32)
    return bits ^ ((bits >> 31) & 0x7FFFFFFF)


def _select_threshold(key_ref, n_chunks, ktarget, pos0, rows, unroll=False):
    ck = key_ref.shape[-1]

    def count(pred):
        def body(c, cnt):
            blk = key_ref[c]
            pos = pos0(c) + lax.broadcasted_iota(I32, (rows, ck), 1)
            m = jnp.where(pred(blk, pos), 1, 0)
            parts = [m[:, n * LANES:(n + 1) * LANES] for n in range(ck // LANES)]
            while len(parts) > 1:
                parts = [parts[p] + parts[p + 1] for p in range(0, len(parts), 2)]
            return cnt + parts[0]
        cnt = lax.fori_loop(0, n_chunks, body, jnp.zeros((rows, LANES), I32), unroll=unroll)
        return jnp.sum(cnt, axis=-1, keepdims=True)

    def count_ge(thr):
        return count(lambda blk, pos: blk >= thr)

    thr = jnp.where(count_ge(jnp.zeros((rows, 1), I32)) >= ktarget, 0, INT_MIN).astype(I32)

    def bit_step(it, thr):
        cand = thr | jnp.left_shift(jnp.int32(1), 30 - it)
        return jnp.where(count_ge(cand) >= ktarget, cand, thr)

    thr = lax.fori_loop(0, 31, bit_step, thr)
    n_gt = count(lambda blk, pos: blk > thr)
    n_ge = count_ge(thr)
    need = ktarget - n_gt
    extra = jnp.max(n_ge - n_gt - need)

    def tie_cut():
        def step(_, lohi):
            lo, hi = lohi
            mid = lo + ((hi - lo) >> 1)
            c = count(lambda blk, pos: (blk == thr) & (pos <= mid))
            ok = c >= need
            return jnp.where(ok, lo, mid), jnp.where(ok, mid, hi)
        lo = jnp.full((rows, 1), -1, I32)
        hi = jnp.full((rows, 1), 2 ** 30, I32)
        _, hi = lax.fori_loop(0, 31, step, (lo, hi))
        return hi

    cut = lax.cond(extra > 0, tie_cut, lambda: jnp.full((rows, 1), 2 ** 30, I32))
    return thr, cut


ROW_BLOCK = 256
SWEEP_UNROLL = 8


def _bit_transpose32(words):
    a = list(words)
    j, m = 16, 0x0000FFFF
    while j:
        for k in range(32):
            if k & j == 0:
                t = (a[k] ^ lax.shift_right_logical(a[k + j], jnp.int32(j))) & m
                a[k] = a[k] ^ t
                a[k + j] = a[k + j] ^ (t << j)
        j >>= 1
        m = m ^ (m << j)
    return a


def _idx_kernel(iqt_ref, wt_ref, ki_ref, bias_ref, p_ref, c_ref, g_ref, *, tq, ck, k_sel):
    i = pl.program_id(0)
    T = ki_ref.shape[0]
    RB = ROW_BLOCK
    n_all = T // ck
    n_chunks = ((i + 1) * tq + ck - 1) // ck
    n_rb = n_chunks * (ck // RB)
    t_pos = i * tq + lax.broadcasted_iota(I32, (1, tq), 1)
    w = wt_ref[...]

    @pl.when(i == 0)
    def _():
        c_ref[...] = jnp.zeros(c_ref.shape, I32)
        g_ref[...] = jnp.zeros(g_ref.shape, I32)
        p_ref[...] = jnp.zeros(p_ref.shape, I32)

    def score_chunk(c, carry):
        r0 = pl.multiple_of(c * ck, ck)
        kc = ki_ref[pl.ds(r0, ck), :]
        acc = jnp.zeros((ck, tq), F32)
        for h in range(IDX_HEADS):
            d = jnp.dot(kc, iqt_ref[h * IDX_DIM:(h + 1) * IDX_DIM, :], preferred_element_type=F32)
            acc = acc + jnp.maximum(d, 0.0) * w[h:h + 1, :]
        s_pos = r0 + lax.broadcasted_iota(I32, (ck, tq), 0)
        bits = lax.bitcast_convert_type(jnp.where(s_pos <= t_pos, acc, -jnp.inf) + 0.0, I32)
        key = bits ^ ((bits >> 31) | jnp.int32(INT_MIN))
        for half in range(ck // RB):
            rb = c * (ck // RB) + half
            base = half * RB
            words = [key[base + 8 * (31 - x):base + 8 * (31 - x) + 8, :] for x in range(32)]
            planes = _bit_transpose32(words)
            for b in range(32):
                p_ref[b, rb] = planes[31 - b]
            c_ref[rb] = jnp.full((8, tq), -1, I32)
            g_ref[rb] = jnp.zeros((8, tq), I32)
        return carry

    lax.fori_loop(0, n_chunks, score_chunk, 0)

    def lane_total(cnt):
        return jnp.sum(cnt, axis=0, keepdims=True)

    def sweep_row_blocks(per_block):
        unroll = math.gcd(SWEEP_UNROLL, T // RB)

        def body(it, cnt):
            for u in range(unroll):
                cnt = cnt + per_block(it * unroll + u)
            return cnt
        n_it = (n_rb + unroll - 1) // unroll
        return lane_total(lax.fori_loop(0, n_it, body, jnp.zeros((8, tq), I32)))

    def decide(ones_count, k_rem):
        keep_ones = ones_count >= k_rem
        return jnp.where(keep_ones, k_rem, k_rem - ones_count), jnp.where(keep_ones, 0, -1)

    def apply_decision(s, b_prev, flip):
        cs = c_ref[s]
        pp = p_ref[b_prev, s]
        g_ref[s] = g_ref[s] | (cs & pp & flip)
        cs = cs & (pp ^ flip)
        c_ref[s] = cs
        return cs

    top = sweep_row_blocks(lambda s: lax.population_count(c_ref[s] & p_ref[31, s]))
    k_rem, flip = decide(top, jnp.minimum(t_pos + 1, k_sel))

    def sweep(it, state):
        k_rem, flip = state
        b = 30 - it
        ones = sweep_row_blocks(lambda s: lax.population_count(apply_decision(s, b + 1, flip) & p_ref[b, s]))
        return decide(ones, k_rem)

    k_rem, flip = lax.fori_loop(0, 31, sweep, (k_rem, flip))
    n_tie = sweep_row_blocks(lambda s: lax.population_count(apply_decision(s, 0, flip)))
    extra = jnp.max(n_tie - k_rem)

    def write_bias(sel_fn):
        def body(s, carry):
            sel = sel_fn(s)
            for jp in range(16):
                rows = jnp.concatenate([(sel >> (2 * jp)) & 1, (sel >> (2 * jp + 1)) & 1], axis=0)
                r0 = pl.multiple_of(s * RB + 16 * jp, 16)
                bias_ref[0, pl.ds(r0, 16), :] = jnp.where(rows != 0, 0.0, MASKED).astype(bias_ref.dtype)
            return carry
        lax.fori_loop(0, n_rb, body, 0)

    def no_surplus_ties():
        write_bias(lambda s: g_ref[s] | c_ref[s])

    def surplus_ties():
        sub = lax.broadcasted_iota(I32, (8, tq), 0)

        def upto(s, cut):
            nb = jnp.clip(((cut - s * RB - sub) >> 3) + 1, 0, 32)
            return jnp.where(nb >= 32, -1, (jnp.int32(1) << jnp.minimum(nb, 31)) - 1)

        def step(_, lohi):
            lo, hi = lohi
            mid = lo + ((hi - lo) >> 1)
            c = sweep_row_blocks(lambda s: lax.population_count(c_ref[s] & upto(s, mid)))
            ok = c >= k_rem
            return jnp.where(ok, lo, mid), jnp.where(ok, mid, hi)

        _, cut = lax.fori_loop(0, 15, step, (jnp.full((1, tq), -1, I32), jnp.full((1, tq), T - 1, I32)))
        write_bias(lambda s: g_ref[s] | (c_ref[s] & upto(s, cut)))

    lax.cond(extra > 0, surplus_ties, no_surplus_ties)

    def write_rest(c, carry):
        r0 = pl.multiple_of(c * ck, ck)
        bias_ref[0, pl.ds(r0, ck), :] = jnp.full((ck, tq), MASKED, bias_ref.dtype)
        return carry

    lax.fori_loop(n_chunks, n_all, write_rest, 0)


def idx_select(ct, wt, ki, T, tq, ck):
    k_sel = min(TOPK_MAX, T // 4)
    return pl.pallas_call(
        functools.partial(_idx_kernel, tq=tq, ck=ck, k_sel=k_sel),
        grid=(T // tq,),
        in_specs=[pl.BlockSpec((IDX_HEADS * IDX_DIM, tq), lambda i: (CT_IQ // (IDX_HEADS * IDX_DIM), i)),
                  pl.BlockSpec((IDX_HEADS, tq), lambda i: (0, i)),
                  pl.BlockSpec((T, IDX_DIM), lambda i: (0, 0))],
        out_specs=pl.BlockSpec((1, T, tq), lambda i: (i, 0, 0)),
        out_shape=jax.ShapeDtypeStruct((T // tq, T, tq), BF16),
        scratch_shapes=[pltpu.VMEM((32, T // ROW_BLOCK, 8, tq), I32),
                        pltpu.VMEM((T // ROW_BLOCK, 8, tq), I32), pltpu.VMEM((T // ROW_BLOCK, 8, tq), I32)],
        compiler_params=pltpu.CompilerParams(dimension_semantics=("arbitrary",), vmem_limit_bytes=IDX_VMEM_LIMIT),
        name="idx_select",
    )(ct, wt, ki)


DSA_DIM_AUG = DSA_DIM + 16
LOG2E = 1.4426950408889634
QK_AHEAD = 2


def _attn_kernel(ii_ref, jj_ref, qt_ref, k_ref, vt_ref, b_ref, o_ref, m_ref, l_ref, acc_ref, *, tq, tk):
    H, D, DA = DSA_HEADS, DSA_DIM, DSA_DIM_AUG
    s_id = pl.program_id(0)
    i, j = ii_ref[s_id], jj_ref[s_id]

    @pl.when(j == 0)
    def _():
        m_ref[...] = jnp.full(m_ref.shape, M_INIT, F32)
        l_ref[...] = jnp.zeros(l_ref.shape, F32)
        acc_ref[...] = jnp.zeros(acc_ref.shape, F32)

    bias = jnp.concatenate([b_ref[n] for n in range(b_ref.shape[0])], axis=1).astype(F32)

    def qk(h):
        return jnp.dot(k_ref[:, h * D:(h + 1) * D], qt_ref[h * D:(h + 1) * D, :], preferred_element_type=F32)

    pending = [qk(h) for h in range(QK_AHEAD)]
    for h in range(H):
        s = pending.pop(0)
        if h + QK_AHEAD < H:
            pending.append(qk(h + QK_AHEAD))
        s = s * (D ** -0.5 * LOG2E) + bias
        m_prev = m_ref[h:h + 1, :]
        m_new = jnp.maximum(m_prev, jnp.max(s, axis=0, keepdims=True))
        alpha = jnp.exp2(m_prev - m_new)
        pr = jnp.exp2((s - m_new).astype(BF16))
        pv = jnp.dot(vt_ref[h * DA:(h + 1) * DA, :], pr, preferred_element_type=F32)
        acc_ref[h * D:(h + 1) * D, :] = alpha * acc_ref[h * D:(h + 1) * D, :] + pv[:D]
        l_ref[h:h + 1, :] = alpha * l_ref[h:h + 1, :] + pv[D:D + 1]
        m_ref[h:h + 1, :] = m_new

    @pl.when(j == (i * tq + tq - 1) // tk)
    def _():
        for h in range(H):
            o_ref[:, h * D:(h + 1) * D] = (acc_ref[h * D:(h + 1) * D, :] / l_ref[h:h + 1, :]).T.astype(o_ref.dtype)


def attention(ct, k, vt_aug, bias, T, tq, tk):
    H, D, DA = DSA_HEADS, DSA_DIM, DSA_DIM_AUG
    tqb = bias.shape[2]
    pairs = [(i, j) for i in range(T // tq) for j in range((i * tq + tq - 1) // tk + 1)]
    ii = jnp.asarray([p[0] for p in pairs], I32)
    jj = jnp.asarray([p[1] for p in pairs], I32)
    grid_spec = pltpu.PrefetchScalarGridSpec(
        num_scalar_prefetch=2,
        grid=(len(pairs),),
        in_specs=[pl.BlockSpec((H * D, tq), lambda s, ii, jj: (CT_AQ // (H * D), ii[s])),
                  pl.BlockSpec((tk, H * D), lambda s, ii, jj: (jj[s], 0)),
                  pl.BlockSpec((H * DA, tk), lambda s, ii, jj: (0, jj[s])),
                  pl.BlockSpec((tq // tqb, tk, tqb), lambda s, ii, jj: (ii[s], jj[s], 0))],
        out_specs=pl.BlockSpec((tq, H * D), lambda s, ii, jj: (ii[s], 0)),
        scratch_shapes=[pltpu.VMEM((H, tq), F32), pltpu.VMEM((H, tq), F32), pltpu.VMEM((H * D, tq), F32)],
    )
    return pl.pallas_call(
        functools.partial(_attn_kernel, tq=tq, tk=tk),
        grid_spec=grid_spec,
        out_shape=jax.ShapeDtypeStruct((T, H * D), BF16),
        compiler_params=_cparams(("arbitrary",)),
        name="attention",
    )(ii, jj, ct, k, vt_aug, bias)


def _idx_sample_kernel(pt_ref, q_ref, w_ref, kn_ref, cache_ref, bias_ref, kbuf_ref, key_ref, sem, *,
                       n_pages, ck, k_sel, t_dec):
    b = pl.program_id(0)
    n_chunks = key_ref.shape[0]

    slot = b % 2

    def page_copy(seq, sl, pg):
        return pltpu.make_async_copy(cache_ref.at[pt_ref[seq, pg]],
                                     kbuf_ref.at[sl, :, pl.ds(pl.multiple_of(pg * PAGE, PAGE), PAGE)], sem.at[sl])

    def start_pages(seq, sl):
        def start(pg, c):
            page_copy(seq, sl, pg).start()
            return c
        lax.fori_loop(0, n_pages, start, 0)

    @pl.when(b == 0)
    def _():
        start_pages(0, 0)

    @pl.when(b + 1 < pl.num_programs(0))
    def _():
        start_pages(b + 1, 1 - slot)

    kbuf_ref[slot, :, n_pages * PAGE:(n_pages + 1) * PAGE] = kn_ref[0]
    kbuf_ref[slot, :, (n_pages + 1) * PAGE:] = jnp.zeros((IDX_DIM, ck - PAGE), F32)

    def wait(pg, c):
        page_copy(b, slot, pg).wait()
        return c

    lax.fori_loop(0, n_pages, wait, 0)

    q = q_ref[0]
    wcol = w_ref[0]
    t_pos = n_pages * PAGE + lax.broadcasted_iota(I32, (t_dec, 1), 0)

    def score_chunk(c, carry):
        kc = kbuf_ref[slot, :, pl.ds(pl.multiple_of(c * ck, ck), ck)].astype(BF16)
        d = jnp.dot(q, kc, preferred_element_type=F32)
        d = jnp.maximum(d, 0.0) * wcol
        sc = d[0:t_dec]
        for h in range(1, IDX_HEADS):
            sc = sc + d[h * t_dec:(h + 1) * t_dec]
        s_pos = c * ck + lax.broadcasted_iota(I32, (t_dec, ck), 1)
        key_ref[c] = _sortable_key(jnp.where(s_pos <= t_pos, sc, -jnp.inf))
        return carry

    lax.fori_loop(0, n_chunks, score_chunk, 0)

    ktarget = jnp.full((t_dec, 1), k_sel, I32)
    thr, cut = _select_threshold(key_ref, n_chunks, ktarget, lambda c: c * ck, t_dec, unroll=True)

    def write_sel(c, carry):
        blk = key_ref[c]
        pos = c * ck + lax.broadcasted_iota(I32, (t_dec, ck), 1)
        sel = (blk > thr) | ((blk == thr) & (pos <= cut))
        bias_ref[0, c] = jnp.where(sel, 0.0, MASKED)
        return carry

    lax.fori_loop(0, n_chunks, write_sel, 0)


def idx_select_sample(iq, iw, kn, cache_idx_kt, page_table, B, t_dec, ck):
    n_pages = page_table.shape[1]
    past = n_pages * PAGE
    k_sel = min(TOPK_MAX, (past + t_dec) // 4)
    n_chunks = past // ck + 1
    iw = iw.reshape(B, t_dec, IDX_HEADS)
    iw = (iw * (IDX_HEADS ** -0.5 * IDX_DIM ** -0.5)).transpose(0, 2, 1).reshape(B, IDX_HEADS * t_dec, 1)
    kn = jnp.pad(kn.reshape(B, t_dec, IDX_DIM).transpose(0, 2, 1), ((0, 0), (0, 0), (0, PAGE - t_dec)))
    grid_spec = pltpu.PrefetchScalarGridSpec(
        num_scalar_prefetch=1,
        grid=(B,),
        in_specs=[pl.BlockSpec((1, IDX_HEADS * t_dec, IDX_DIM), lambda b, pt: (b, 0, 0)),
                  pl.BlockSpec((1, IDX_HEADS * t_dec, 1), lambda b, pt: (b, 0, 0)),
                  pl.BlockSpec((1, IDX_DIM, PAGE), lambda b, pt: (b, 0, 0)),
                  pl.BlockSpec(memory_space=pl.ANY)],
        out_specs=pl.BlockSpec((1, n_chunks, t_dec, ck), lambda b, pt: (b, 0, 0, 0)),
        scratch_shapes=[pltpu.VMEM((2, IDX_DIM, past + ck), F32),
                        pltpu.VMEM((n_chunks, t_dec, ck), I32),
                        pltpu.SemaphoreType.DMA((2,))],
    )
    return pl.pallas_call(
        functools.partial(_idx_sample_kernel, n_pages=n_pages, ck=ck, k_sel=k_sel, t_dec=t_dec),
        grid_spec=grid_spec,
        out_shape=jax.ShapeDtypeStruct((B, n_chunks, t_dec, ck), F32),
        compiler_params=_cparams(("arbitrary",)),
        name="idx_select_sample",
    )(page_table, iq, iw, kn, cache_idx_kt)


KEY_GROUP = 64


def _attn_sample_kernel(pt_ref, q_ref, kn_ref, vn_ref, b_ref, hm_ref, ex_ref, *refs, pg, t_dec):
    H, D = DSA_HEADS, DSA_DIM
    k_refs, v_refs = refs[:pg], refs[pg:2 * pg]
    o_ref, m_ref, l_ref, acc_ref = refs[2 * pg:]
    j = pl.program_id(1)
    nj = pl.num_programs(1)
    n_groups = pg * PAGE // KEY_GROUP

    @pl.when(j == 0)
    def _():
        m_ref[...] = jnp.full(m_ref.shape, M_INIT, F32)
        l_ref[...] = jnp.zeros(l_ref.shape, F32)
        acc_ref[...] = jnp.zeros(acc_ref.shape, F32)

    def update(kc, vc):
        b = b_ref[0, 0]
        by_group = jnp.concatenate([b[:, g * KEY_GROUP:(g + 1) * KEY_GROUP] for g in range(n_groups)], axis=0)
        wide = jnp.dot(by_group.astype(BF16), ex_ref[...], preferred_element_type=F32)
        hm = hm_ref[...]
        bias = jnp.concatenate(
            [jnp.concatenate([wide[g * t_dec:(g + 1) * t_dec]] * H, axis=0) + hm for g in range(n_groups)], axis=1)
        s = lax.dot_general(q_ref[0], kc, (((1,), (1,)), ((), ())), preferred_element_type=F32)
        s = s * (D ** -0.5) + bias
        m_prev = m_ref[...]
        m_new = jnp.maximum(m_prev, jnp.max(s, axis=-1, keepdims=True))
        alpha = jnp.exp(m_prev - m_new)
        pr = jnp.exp(s - m_new)
        l_ref[...] = alpha * l_ref[...] + jnp.sum(pr, axis=-1, keepdims=True)
        acc_ref[...] = alpha * acc_ref[...] + jnp.dot(pr.astype(BF16), vc, preferred_element_type=F32)
        m_ref[...] = m_new

    @pl.when(j < nj - 1)
    def _():
        kc = jnp.concatenate([r[0] for r in k_refs], axis=0).astype(BF16)
        vc = jnp.concatenate([r[0] for r in v_refs], axis=0).astype(BF16)
        update(kc, vc)

    @pl.when(j == nj - 1)
    def _():
        pad = jnp.zeros(((pg * PAGE - t_dec) * H, D), F32)
        kc = jnp.concatenate([kn_ref[0], pad], axis=0).astype(BF16)
        vc = jnp.concatenate([vn_ref[0], pad], axis=0).astype(BF16)
        update(kc, vc)
        out = acc_ref[...] / l_ref[...]
        for h in range(H):
            o_ref[0, :, h * D:(h + 1) * D] = out[h * t_dec:(h + 1) * t_dec].astype(o_ref.dtype)


def attention_sample(q, kn, vn, cache_k, cache_v, page_table, bias, B, t_dec, pg):
    H, D = DSA_HEADS, DSA_DIM
    n_pages = page_table.shape[1]
    nj = n_pages // pg + 1
    gl = KEY_GROUP * H
    q_head = jnp.arange(H * t_dec, dtype=I32)[:, None] // t_dec
    k_head = jnp.arange(gl, dtype=I32)[None, :] % H
    head_mask = jnp.where(q_head == k_head, 0.0, MASKED).astype(F32)
    expand = (jnp.arange(KEY_GROUP, dtype=I32)[:, None] == jnp.arange(gl, dtype=I32)[None, :] // H).astype(BF16)

    def page_map(u):
        return lambda b, j, pt: (pt[b, jnp.minimum(j, n_pages // pg - 1) * pg + u], 0, 0)

    page_specs = [pl.BlockSpec((1, PAGE * H, D), page_map(u)) for u in range(pg)]
    grid_spec = pltpu.PrefetchScalarGridSpec(
        num_scalar_prefetch=1,
        grid=(B, nj),
        in_specs=[pl.BlockSpec((1, H * t_dec, D), lambda b, j, pt: (b, 0, 0)),
                  pl.BlockSpec((1, t_dec * H, D), lambda b, j, pt: (b, 0, 0)),
                  pl.BlockSpec((1, t_dec * H, D), lambda b, j, pt: (b, 0, 0)),
                  pl.BlockSpec((1, 1, t_dec, pg * PAGE), lambda b, j, pt: (b, j, 0, 0)),
                  pl.BlockSpec((H * t_dec, gl), lambda b, j, pt: (0, 0)),
                  pl.BlockSpec((KEY_GROUP, gl), lambda b, j, pt: (0, 0))]
        + page_specs + page_specs,
        out_specs=pl.BlockSpec((1, t_dec, H * D), lambda b, j, pt: (b, 0, 0)),
        scratch_shapes=[pltpu.VMEM((H * t_dec, 1), F32), pltpu.VMEM((H * t_dec, 1), F32),
                        pltpu.VMEM((H * t_dec, D), F32)],
    )
    return pl.pallas_call(
        functools.partial(_attn_sample_kernel, pg=pg, t_dec=t_dec),
        grid_spec=grid_spec,
        out_shape=jax.ShapeDtypeStruct((B, t_dec, H * D), F32),
        compiler_params=_cparams(("parallel", "arbitrary")),
        name="attention_sample",
    )(page_table, q, kn, vn, bias, head_mask, expand, *([cache_k] * pg), *([cache_v] * pg))


def _out_proj_kernel(x_ref, a_ref, b_ref, wa_ref, wb_ref, o_ref):
    o_ref[...] = (x_ref[...] + jnp.dot(a_ref[...], wa_ref[...], preferred_element_type=F32)
                  + jnp.dot(b_ref[...], wb_ref[...], preferred_element_type=F32))


def out_proj(x, a, b, wa, wb, tm, tn):
    M, D = x.shape
    Ka, Kb = a.shape[1], b.shape[1]
    return pl.pallas_call(
        _out_proj_kernel,
        grid=(M // tm, D // tn),
        in_specs=[pl.BlockSpec((tm, tn), lambda i, j: (i, j)),
                  pl.BlockSpec((tm, Ka), lambda i, j: (i, 0)),
                  pl.BlockSpec((tm, Kb), lambda i, j: (i, 0)),
                  pl.BlockSpec((Ka, tn), lambda i, j: (0, j)),
                  pl.BlockSpec((Kb, tn), lambda i, j: (0, j))],
        out_specs=pl.BlockSpec((tm, tn), lambda i, j: (i, j)),
        out_shape=jax.ShapeDtypeStruct((M, D), F32),
        compiler_params=_cparams(("parallel", "arbitrary")),
        name="out_proj",
    )(x, a, b, wa, wb)


def _ffn_kernel(x_ref, g_ref, wu_ref, wd_ref, gf_ref, o_ref, h_ref, acc_ref):
    f = pl.program_id(1)

    @pl.when(f == 0)
    def _():
        x = x_ref[...]
        ms = jnp.mean(x * x, axis=-1, keepdims=True)
        h_ref[...] = (x * lax.rsqrt(ms + EPS) * g_ref[...]).astype(BF16)
        acc_ref[...] = jnp.zeros(acc_ref.shape, F32)

    u = jnp.dot(h_ref[...], wu_ref[...], preferred_element_type=F32)
    u = jnp.square(jnp.maximum(u, 0.0)).astype(BF16)
    acc_ref[...] += jnp.dot(u, wd_ref[...], preferred_element_type=F32)

    @pl.when(f == pl.num_programs(1) - 1)
    def _():
        x2 = x_ref[...] + acc_ref[...]
        ms = jnp.mean(x2 * x2, axis=-1, keepdims=True)
        o_ref[...] = x2 * lax.rsqrt(ms + EPS) * gf_ref[...]


def ffn(x, g_ffn, w_up, w_down, g_final, tm, tf):
    M, D = x.shape
    Fd = w_up.shape[1]
    return pl.pallas_call(
        _ffn_kernel,
        grid=(M // tm, Fd // tf),
        in_specs=[pl.BlockSpec((tm, D), lambda i, f: (i, 0)),
                  pl.BlockSpec((1, D), lambda i, f: (0, 0)),
                  pl.BlockSpec((D, tf), lambda i, f: (0, f)),
                  pl.BlockSpec((tf, D), lambda i, f: (f, 0)),
                  pl.BlockSpec((1, D), lambda i, f: (0, 0))],
        out_specs=pl.BlockSpec((tm, D), lambda i, f: (i, 0)),
        out_shape=jax.ShapeDtypeStruct((M, D), F32),
        scratch_shapes=[pltpu.VMEM((tm, D), BF16), pltpu.VMEM((tm, D), F32)],
        compiler_params=_cparams(("parallel", "arbitrary")),
        name="ffn",
    )(x, g_ffn.reshape(1, D), w_up, w_down, g_final.reshape(1, D))


def _pack_w_in(w_in):
    D = w_in.shape[0]
    gq, gk, gv, gg, ga, aq, ak, av, iq, iw, ik = jnp.split(
        w_in, [512, 1024, 2048, 3072, 3088, 4112, 5136, 6160, 7184, 7200], axis=1)
    pad = jnp.zeros((D, PA_WIDTH - C_SMALL - 96), w_in.dtype)
    w_a = jnp.concatenate([gq, gk, gv, gg, ik, ga, iw, pad], axis=1).astype(BF16)
    w_b = jnp.concatenate([ak, av], axis=1).astype(BF16)
    w_ct = jnp.concatenate([aq, av, iq], axis=1).T.astype(BF16)
    return w_a, w_b, w_ct


def _tile(n, pref):
    return pref if n % pref == 0 else n


def kernel(x_prompt, x_sample, cache_k, cache_v, cache_idx_k, state_gla, page_table,
           g_mix, w_in, w_gate_up, b_gate, g_gla_out, w_o, g_ffn, w_up, w_down, g_final):
    depth = w_in.shape[0]
    Bp, Tp, D = x_prompt.shape
    Bs, Ts, _ = x_sample.shape
    assert Bp == 1 and depth == 1
    H, Dh = DSA_HEADS, DSA_DIM
    n_pool = cache_k.shape[1]

    xp = x_prompt.reshape(Bp * Tp, D)
    xs = x_sample.reshape(Bs * Ts, D)
    l = 0
    w_a, w_b, w_ct = _pack_w_in(w_in[l])
    wo_a = w_o[l, :GLA_HEADS * GLA_DV].astype(BF16)
    wo_b = w_o[l, GLA_HEADS * GLA_DV:].astype(BF16)
    wu = w_up[l].astype(BF16)
    wd = w_down[l].astype(BF16)
    idx_scale = IDX_HEADS ** -0.5 * IDX_DIM ** -0.5

    tm = _tile(Tp, 512)
    pa_p = norm_matmul(xp, g_mix[l], w_a, tm, PA_WIDTH // 2)
    k3_p, v3_p, kb_p = norm_matmul_heads(xp, g_mix[l], w_b, tm)
    ct_p = norm_matmul_t(xp, g_mix[l], w_ct, tm, 1024)
    s0 = jnp.zeros((Bp, GLA_HEADS, GLA_DK, GLA_DV), F32)
    o_gla_p, s_fin_p = gla(pa_p, w_gate_up[l], b_gate[l], g_gla_out[l], s0, Bp, Tp, _tile(Tp, 128), 16)
    ik_p = pa_p[:, C_SMALL + SM_IK:C_SMALL + SM_IK + IDX_DIM]
    wt = (pa_p[:, C_SMALL + SM_IW:C_SMALL + SM_IW + IDX_HEADS] * idx_scale).T
    bias = idx_select(ct_p, wt, ik_p.astype(BF16), Tp, _tile(Tp, 256), _tile(Tp, 512))
    vt = ct_p[CT_AV:CT_AV + H * Dh].reshape(H, Dh, Tp)
    vt_aug = jnp.concatenate([vt, jnp.ones((H, DSA_DIM_AUG - Dh, Tp), BF16)], axis=1).reshape(H * DSA_DIM_AUG, Tp)
    o_dsa_p = attention(ct_p, kb_p, vt_aug, bias, Tp, tm, _tile(Tp, 1024))
    x1p = out_proj(xp, o_gla_p, o_dsa_p, wo_a, wo_b, tm, 1024)
    y_p = ffn(x1p, g_ffn[l], wu, wd, g_final, tm, 1024)

    Ms = Bs * Ts
    pa_s = norm_matmul(xs, g_mix[l], w_a, Ms, PA_WIDTH // 2)
    k3_s, v3_s, _ = norm_matmul_heads(xs, g_mix[l], w_b, Ms)
    ct_s = norm_matmul_t(xs, g_mix[l], w_ct, Ms, 1024)
    o_gla_s, s_fin_s = gla(pa_s, w_gate_up[l], b_gate[l], g_gla_out[l], state_gla[l], Bs, Ts, Ts, Ts)
    ik_s = pa_s[:, C_SMALL + SM_IK:C_SMALL + SM_IK + IDX_DIM]
    iq_s = ct_s[CT_IQ:CT_IQ + IDX_HEADS * IDX_DIM].reshape(IDX_HEADS, IDX_DIM, Bs, Ts)
    iq_s = iq_s.transpose(2, 0, 3, 1).reshape(Bs, IDX_HEADS * Ts, IDX_DIM)
    pg = 8
    bias_s = idx_select_sample(iq_s, pa_s[:, C_SMALL + SM_IW:C_SMALL + SM_IW + IDX_HEADS], ik_s,
                               jnp.swapaxes(cache_idx_k[l], 1, 2), page_table, Bs, Ts, pg * PAGE)
    q_s = ct_s[CT_AQ:CT_AQ + H * Dh].reshape(H, Dh, Bs, Ts).transpose(2, 0, 3, 1).reshape(Bs, H * Ts, Dh)
    o_dsa_s = attention_sample(q_s, k3_s.reshape(Bs, Ts * H, Dh), v3_s.reshape(Bs, Ts * H, Dh),
                               cache_k[l].reshape(n_pool, PAGE * H, Dh), cache_v[l].reshape(n_pool, PAGE * H, Dh),
                               page_table, bias_s, Bs, Ts, pg)
    x1s = out_proj(xs, o_gla_s, o_dsa_s.reshape(Ms, H * Dh).astype(BF16), wo_a, wo_b, Ms, 1024)
    y_s = ffn(x1s, g_ffn[l], wu, wd, g_final, Ms, 512)

    return (y_p.reshape(Bp, Tp, D), y_s.reshape(Bs, Ts, D),
            k3_p.reshape(1, Bp, Tp, H, Dh), v3_p.reshape(1, Bp, Tp, H, Dh),
            ik_p.reshape(1, Bp, Tp, IDX_DIM), s_fin_p[None],
            k3_s.reshape(1, Bs, Ts, H, Dh), v3_s.reshape(1, Bs, Ts, H, Dh),
            ik_s.reshape(1, Bs, Ts, IDX_DIM), s_fin_s[None])
```
